```python
import math
import jax, jax.numpy as jnp
from jax import lax
import numpy as np

D_MODEL = 1024
BATCH = 8
SEQ = 2048
DEPTH = 4
DEC_BATCH = 128
DEC_SEQ = 4
PAST_LEN = 16384
PAGE_SIZE = 128

D_PLE = 256
HG_HEADS = 6
HG_DK = 64
HG_DV = 64
D_HG = HG_HEADS * HG_DV
HG_CHUNK = 64
LB_FLOOR = 1e-12
RW_HEADS = 6
RW_HD = 64
D_RW = RW_HEADS * RW_HD
RW_LORA_W = 64
RW_LORA_A = 64
D_SHIFT = 3 * D_RW + RW_LORA_W + RW_LORA_A
S5_GROUPS = 16
S5_CH = 16
S5_P = 64
D_S5 = S5_GROUPS * S5_CH
D_MIX = D_HG + D_RW + D_S5
D_IN = 4 * D_HG + D_SHIFT + D_RW + 2 * D_S5
IN_SPLITS = (D_HG, 2 * D_HG, 3 * D_HG, 4 * D_HG, 4 * D_HG + D_SHIFT, 4 * D_HG + D_SHIFT + D_RW, 4 * D_HG + D_SHIFT + D_RW + D_S5)
RW_SPLITS = (D_RW, 2 * D_RW, 3 * D_RW, 3 * D_RW + RW_LORA_W)
NORM_EPS = 1e-6
RW_GN_EPS = 64e-5

kernel_name = 'hymba_hgrn2_rwkv7_s5_step'


def _rmsnorm(x, g):
    xf = x.astype(jnp.float32)
    y = xf * lax.rsqrt(jnp.mean(xf * xf, axis=-1, keepdims=True) + NORM_EPS)
    return y * g.astype(jnp.float32)


def _hgrn2(q, f, i, lb, s0):
    B, L, _ = q.shape
    f32 = jnp.float32
    qf = jax.nn.silu(q.astype(f32))
    lb_safe = jnp.maximum(lb, LB_FLOOR)
    logf = jnp.logaddexp(jnp.log(lb_safe), jnp.log1p(-lb_safe) + jax.nn.log_sigmoid(f.astype(f32)))
    kf = -jnp.expm1(logf)
    C = HG_CHUNK if L % HG_CHUNK == 0 else L
    n = L // C

    def chunks(t):
        return t.reshape(B, n, C, HG_HEADS, -1).transpose(1, 0, 3, 2, 4)

    causal = jnp.tril(jnp.ones((C, C), dtype=bool))[None, None, :, :, None]

    def step(S, inp):
        qc, kc, ic, gc = inp
        b = jnp.cumsum(gc, axis=2)
        diff = jnp.minimum(b[:, :, :, None, :] - b[:, :, None, :, :], 0.0)
        rel = jnp.where(causal, jnp.exp(diff), 0.0)
        att = jnp.einsum('bhtd,bhsd,bhtsd->bhts', qc, kc, rel)
        o = jnp.einsum('bhts,bhsv->bhtv', att, ic) + jnp.einsum('bhtd,bhdv->bhtv', qc * jnp.exp(b), S)
        b_end = b[:, :, -1:, :]
        S = jnp.exp(b_end[:, :, 0, :, None]) * S + jnp.einsum('bhsd,bhsv->bhdv', kc * jnp.exp(b_end - b), ic)
        return S, o

    S, o = lax.scan(step, s0.astype(f32), (chunks(qf), chunks(kf), chunks(i.astype(f32)), chunks(logf)))
    o = o.transpose(1, 0, 3, 2, 4).reshape(B, L, D_HG)
    return o, S


def _rwkv7(c, shift_prev, s0, mu, w0, w_up, a0, a_up, k_k, k_a, r_k, gn_g, gn_b):
    B, L, _ = c.shape
    f32 = jnp.float32
    c = c.astype(f32)
    prev = jnp.concatenate([shift_prev.astype(f32)[:, None, :], c[:, :-1]], axis=1)
    cs = c + mu * (prev - c)
    r, k, v, wd, ad = jnp.split(cs, RW_SPLITS, axis=-1)
    w_log = -jax.nn.softplus(-(w0 + jnp.tanh(wd) @ w_up)) - 0.5
    decay = jnp.exp(-jnp.exp(w_log))
    a = jax.nn.sigmoid(a0 + ad @ a_up)
    hs = lambda t: t.reshape(B, L, RW_HEADS, RW_HD)
    r, k, v, decay, a = hs(r), hs(k), hs(v), hs(decay), hs(a)
    kk = k * k_k.reshape(RW_HEADS, RW_HD)
    kk = kk * lax.rsqrt(jnp.maximum(jnp.sum(kk * kk, axis=-1, keepdims=True), 1e-24))
    k = k * (1.0 + (a - 1.0) * k_a.reshape(RW_HEADS, RW_HD))

    def step(S, inp):
        r_t, w_t, k_t, v_t, kk_t, a_t = inp
        sa = jnp.einsum('bhvk,bhk->bhv', S, kk_t)
        S = S * w_t[:, :, None, :] - sa[..., None] * (kk_t * a_t)[:, :, None, :] + v_t[..., None] * k_t[:, :, None, :]
        return S, jnp.einsum('bhvk,bhk->bhv', S, r_t)

    tm = lambda t: t.transpose(1, 0, 2, 3)
    S, y = lax.scan(step, s0.astype(f32), (tm(r), tm(decay), tm(k), tm(v), tm(kk), tm(a)))
    y = y.transpose(1, 0, 2, 3)
    mean = jnp.mean(y, axis=-1, keepdims=True)
    var = jnp.mean(jnp.square(y - mean), axis=-1, keepdims=True)
    y = (y - mean) * lax.rsqrt(var + RW_GN_EPS) * gn_g.reshape(RW_HEADS, RW_HD) + gn_b.reshape(RW_HEADS, RW_HD)
    y = y + jnp.sum(r * k * r_k.reshape(RW_HEADS, RW_HD), axis=-1, keepdims=True) * v
    return y.reshape(B, L, D_RW), S, c[:, -1]


def _s5(u, s0_re, s0_im, a_re, a_im, log_dt, b_re, b_im, c_re, c_im, d_skip, w_glu1, w_glu2):
    B, L, _ = u.shape
    f32 = jnp.float32
    ug = u.astype(f32).reshape(B, L, S5_GROUPS, S5_CH)
    dt = jnp.exp(log_dt.astype(f32))[:, None]
    mag = jnp.exp(dt * a_re)
    lam_re, lam_im = mag * jnp.cos(dt * a_im), mag * jnp.sin(dt * a_im)
    den = a_re * a_re + a_im * a_im
    xr = lam_re - 1.0
    f_re = (xr * a_re + lam_im * a_im) / den
    f_im = (lam_im * a_re - xr * a_im) / den
    bb_re = f_re[..., None] * b_re - f_im[..., None] * b_im
    bb_im = f_re[..., None] * b_im + f_im[..., None] * b_re
    bu_re = jnp.einsum('blgc,gpc->blgp', ug, bb_re)
    bu_im = jnp.einsum('blgc,gpc->blgp', ug, bb_im)
    s0_re = s0_re.astype(f32)
    s0_im = s0_im.astype(f32)
    bu_re = bu_re.at[:, 0].add(lam_re * s0_re - lam_im * s0_im)
    bu_im = bu_im.at[:, 0].add(lam_re * s0_im + lam_im * s0_re)
    lr = jnp.broadcast_to(lam_re, bu_re.shape)
    li = jnp.broadcast_to(lam_im, bu_im.shape)

    def combine(e1, e2):
        a1r, a1i, b1r, b1i = e1
        a2r, a2i, b2r, b2i = e2
        return (a1r * a2r - a1i * a2i, a1r * a2i + a1i * a2r,
                a2r * b1r - a2i * b1i + b2r, a2r * b1i + a2i * b1r + b2i)

    _, _, s_re, s_im = lax.associative_scan(combine, (lr, li, bu_re, bu_im), axis=1)
    y = jnp.einsum('blgp,gcp->blgc', s_re, c_re) - jnp.einsum('blgp,gcp->blgc', s_im, c_im)
    y = (y + d_skip.reshape(S5_GROUPS, S5_CH) * ug).reshape(B, L, D_S5)
    y = jax.nn.gelu(y)
    y = (y @ w_glu1) * jax.nn.sigmoid(y @ w_glu2)
    return y, s_re[:, -1], s_im[:, -1]


def _forward(x, p, st_hg, st_rw, st_sh, st_re, st_im, prm):
    f32 = jnp.float32
    lb_all = jax.nn.softmax(prm['hg_lb_raw'].astype(f32), axis=0)
    lb_all = jnp.cumsum(lb_all, axis=0) - lb_all[0]
    B, L, _ = x.shape
    out_hg, out_rw, out_sh, out_re, out_im = [], [], [], [], []
    x = x.astype(f32)
    for l in range(DEPTH):
        h = _rmsnorm(x, prm['g_in'][l])
        z = h @ prm['w_in'][l]
        q_a, f_a, i_a, gt_a, c_b, gt_b, u_c, gt_c = jnp.split(z, IN_SPLITS, axis=-1)
        o_a, s_a = _hgrn2(q_a, f_a, i_a, lb_all[l], st_hg[l])
        o_a = _rmsnorm(o_a.reshape(B, L, HG_HEADS, HG_DV), prm['hg_norm_g'][l].reshape(HG_HEADS, HG_DV)).reshape(B, L, D_HG)
        o_a = o_a * jax.nn.silu(gt_a)
        o_b, s_b, sh_b = _rwkv7(c_b, st_sh[l], st_rw[l], prm['rw_mu'][l], prm['rw_w0'][l], prm['rw_w_up'][l],
                                prm['rw_a0'][l], prm['rw_a_up'][l], prm['rw_k_k'][l], prm['rw_k_a'][l],
                                prm['rw_r_k'][l], prm['rw_gn_g'][l], prm['rw_gn_b'][l])
        o_b = o_b * jax.nn.silu(gt_b)
        o_c, sr_c, si_c = _s5(u_c, st_re[l], st_im[l], prm['s5_a_re'][l], prm['s5_a_im'][l], prm['s5_log_dt'][l],
                              prm['s5_b_re'][l], prm['s5_b_im'][l], prm['s5_c_re'][l], prm['s5_c_im'][l],
                              prm['s5_d'][l], prm['s5_glu_w1'][l], prm['s5_glu_w2'][l])
        o_c = o_c * jax.nn.silu(gt_c)
        x = x + jnp.concatenate([o_a, o_b, o_c], axis=-1) @ prm['w_out'][l]
        gate = jax.nn.sigmoid(_rmsnorm(x, prm['ple_gate_g'][l]) @ prm['ple_w_gate'][l])
        x = x + (p[l].astype(f32) @ prm['ple_w_proj'][l]) * gate
        out_hg.append(s_a)
        out_rw.append(s_b)
        out_sh.append(sh_b)
        out_re.append(sr_c)
        out_im.append(si_c)
    y = _rmsnorm(x, prm['g_final'])
    return y, jnp.stack(out_hg), jnp.stack(out_rw), jnp.stack(out_sh), jnp.stack(out_re), jnp.stack(out_im)


def setup_inputs(seed: int = 0) -> dict:
    key = jax.random.key(seed)
    ks = iter(jax.random.split(key, 64))
    f32 = jnp.float32
    nrm = lambda shape, scale: scale * jax.random.normal(next(ks), shape, f32)
    unif = lambda shape, lo, hi: jax.random.uniform(next(ks), shape, f32, lo, hi)
    a_im0 = math.pi * jnp.arange(S5_P, dtype=f32)[None, None, :]
    return {
        'x_prompt': nrm((BATCH, SEQ, D_MODEL), 1.0),
        'x_sample': nrm((DEC_BATCH, DEC_SEQ, D_MODEL), 1.0),
        'state_hgrn': nrm((DEPTH, DEC_BATCH, HG_HEADS, HG_DK, HG_DV), 0.1),
        'state_rwkv': nrm((DEPTH, DEC_BATCH, RW_HEADS, RW_HD, RW_HD), 0.1),
        'state_rwkv_shift': nrm((DEPTH, DEC_BATCH, D_SHIFT), 1.0),
        'state_s5_re': nrm((DEPTH, DEC_BATCH, S5_GROUPS, S5_P), 0.3),
        'state_s5_im': nrm((DEPTH, DEC_BATCH, S5_GROUPS, S5_P), 0.3),
        'p_prompt': nrm((DEPTH, BATCH, SEQ, D_PLE), 1.0),
        'p_sample': nrm((DEPTH, DEC_BATCH, DEC_SEQ, D_PLE), 1.0),
        'g_in': 1.0 + nrm((DEPTH, D_MODEL), 0.02),
        'w_in': nrm((DEPTH, D_MODEL, D_IN), D_MODEL ** -0.5),
        'hg_lb_raw': nrm((DEPTH, D_HG), 0.5),
        'hg_norm_g': 1.0 + nrm((DEPTH, D_HG), 0.02),
        'rw_mu': unif((DEPTH, D_SHIFT), 0.0, 1.0),
        'rw_w0': unif((DEPTH, D_RW), -6.0, -1.0),
        'rw_w_up': nrm((DEPTH, RW_LORA_W, D_RW), 0.1),
        'rw_a0': nrm((DEPTH, D_RW), 0.1),
        'rw_a_up': nrm((DEPTH, RW_LORA_A, D_RW), 0.1),
        'rw_k_k': 0.85 + nrm((DEPTH, D_RW), 0.02),
        'rw_k_a': 1.0 + nrm((DEPTH, D_RW), 0.02),
        'rw_r_k': nrm((DEPTH, D_RW), 0.1),
        'rw_gn_g': 1.0 + nrm((DEPTH, D_RW), 0.02),
        'rw_gn_b': nrm((DEPTH, D_RW), 0.02),
        's5_a_re': -0.5 + nrm((DEPTH, S5_GROUPS, S5_P), 0.01),
        's5_a_im': a_im0 + nrm((DEPTH, S5_GROUPS, S5_P), 0.01),
        's5_log_dt': unif((DEPTH, S5_GROUPS), math.log(1e-3), math.log(1e-1)),
        's5_b_re': nrm((DEPTH, S5_GROUPS, S5_P, S5_CH), (2 * S5_CH) ** -0.5),
        's5_b_im': nrm((DEPTH, S5_GROUPS, S5_P, S5_CH), (2 * S5_CH) ** -0.5),
        's5_c_re': nrm((DEPTH, S5_GROUPS, S5_CH, S5_P), S5_P ** -0.5),
        's5_c_im': nrm((DEPTH, S5_GROUPS, S5_CH, S5_P), S5_P ** -0.5),
        's5_d': nrm((DEPTH, D_S5), 1.0),
        's5_glu_w1': nrm((DEPTH, D_S5, D_S5), D_S5 ** -0.5),
        's5_glu_w2': nrm((DEPTH, D_S5, D_S5), D_S5 ** -0.5),
        'w_out': nrm((DEPTH, D_MIX, D_MODEL), D_MIX ** -0.5),
        'ple_w_proj': nrm((DEPTH, D_PLE, D_MODEL), D_PLE ** -0.5),
        'ple_gate_g': 1.0 + nrm((DEPTH, D_MODEL), 0.02),
        'ple_w_gate': nrm((DEPTH, D_MODEL, D_MODEL), D_MODEL ** -0.5),
        'g_final': 1.0 + nrm((D_MODEL,), 0.02),
    }


def reference(x_prompt, x_sample, state_hgrn, state_rwkv, state_rwkv_shift, state_s5_re, state_s5_im,
              p_prompt, p_sample, g_in, w_in, hg_lb_raw, hg_norm_g, rw_mu, rw_w0, rw_w_up, rw_a0, rw_a_up,
              rw_k_k, rw_k_a, rw_r_k, rw_gn_g, rw_gn_b, s5_a_re, s5_a_im, s5_log_dt, s5_b_re, s5_b_im,
              s5_c_re, s5_c_im, s5_d, s5_glu_w1, s5_glu_w2, w_out, ple_w_proj, ple_gate_g, ple_w_gate, g_final):
    prm = {
        'g_in': g_in, 'w_in': w_in, 'hg_lb_raw': hg_lb_raw, 'hg_norm_g': hg_norm_g,
        'rw_mu': rw_mu, 'rw_w0': rw_w0, 'rw_w_up': rw_w_up, 'rw_a0': rw_a0, 'rw_a_up': rw_a_up,
        'rw_k_k': rw_k_k, 'rw_k_a': rw_k_a, 'rw_r_k': rw_r_k, 'rw_gn_g': rw_gn_g, 'rw_gn_b': rw_gn_b,
        's5_a_re': s5_a_re, 's5_a_im': s5_a_im, 's5_log_dt': s5_log_dt, 's5_b_re': s5_b_re, 's5_b_im': s5_b_im,
        's5_c_re': s5_c_re, 's5_c_im': s5_c_im, 's5_d': s5_d, 's5_glu_w1': s5_glu_w1, 's5_glu_w2': s5_glu_w2,
        'w_out': w_out, 'ple_w_proj': ple_w_proj, 'ple_gate_g': ple_gate_g, 'ple_w_gate': ple_w_gate,
        'g_final': g_final,
    }
    B = x_prompt.shape[0]
    f32 = jnp.float32
    z_hg = jnp.zeros((DEPTH, B, HG_HEADS, HG_DK, HG_DV), f32)
    z_rw = jnp.zeros((DEPTH, B, RW_HEADS, RW_HD, RW_HD), f32)
    z_sh = jnp.zeros((DEPTH, B, D_SHIFT), f32)
    z_s5 = jnp.zeros((DEPTH, B, S5_GROUPS, S5_P), f32)
    y_prompt, hg_p, rw_p, sh_p, re_p, im_p = _forward(x_prompt, p_prompt, z_hg, z_rw, z_sh, z_s5, z_s5, prm)
    y_sample, hg_s, rw_s, sh_s, re_s, im_s = _forward(x_sample, p_sample, state_hgrn, state_rwkv, state_rwkv_shift,
                                                      state_s5_re, state_s5_im, prm)
    return (y_prompt, y_sample, hg_p, rw_p, sh_p, re_p, im_p, hg_s, rw_s, sh_s, re_s, im_s)
```

```python
import functools
import math

import jax
import jax.numpy as jnp
import numpy as np
from jax import lax
from jax.experimental import pallas as pl
from jax.experimental.pallas import tpu as pltpu

F32 = jnp.float32
BF16 = jnp.bfloat16

D_MODEL = 1024
D_PLE = 256
HEADS = 6
HD = 64
D_HEADS = HEADS * HD
LORA = 64
D_SHIFT = 3 * D_HEADS + 2 * LORA
S5_GROUPS = 16
S5_CH = 16
S5_P = 64
D_S5 = S5_GROUPS * S5_CH
D_S5_STATE = S5_GROUPS * S5_P
D_IN = 4 * D_HEADS + D_SHIFT + D_HEADS + 2 * D_S5
D_MIX = 2 * D_HEADS + D_S5

C_Q, C_F, C_I, C_GA = 0, D_HEADS, 2 * D_HEADS, 3 * D_HEADS
C_RW = 4 * D_HEADS
C_GB = C_RW + D_SHIFT
C_U = C_GB + D_HEADS
C_GC = C_U + D_S5

LB_FLOOR = 1e-12
NORM_EPS = 1e-6
RW_GN_EPS = 64e-5
RW_DECAY_SCALE = math.exp(-0.5)

CHUNK = 64
HG_BASE = 4
ROW_TILE = 512
PITCH = CHUNK + 8
TPU_V7X_VMEM_LIMIT = 56 * 1024 * 1024


def _dot(a, b):
    return jnp.dot(a.astype(BF16), b.astype(BF16), preferred_element_type=F32)


def _dot_nt(a, b):
    return lax.dot_general(a.astype(BF16), b.astype(BF16), (((1,), (1,)), ((), ())), preferred_element_type=F32)


def _dot_tn(a, b):
    return lax.dot_general(a.astype(BF16), b.astype(BF16), (((0,), (0,)), ((), ())), preferred_element_type=F32)


def _split3(x):
    hi = x.astype(BF16)
    r1 = x - hi.astype(F32)
    mid = r1.astype(BF16)
    lo = (r1 - mid.astype(F32)).astype(BF16)
    return hi, mid, lo


def _sel_left(m, x):
    hi, mid, lo = _split3(x)
    d = lambda p: jnp.dot(m, p, preferred_element_type=F32)
    return d(hi) + d(mid) + d(lo)


def _sel_right(x, m):
    hi = x.astype(BF16)
    lo = (x - hi.astype(F32)).astype(BF16)
    d = lambda p: jnp.dot(p, m, preferred_element_type=F32)
    return d(hi) + d(lo)


def _sigmoid(x):
    return jax.nn.sigmoid(x)


def _silu(x):
    return x * jax.nn.sigmoid(x)


def _rmsnorm(x, g):
    return x * lax.rsqrt(jnp.mean(x * x, axis=-1, keepdims=True) + NORM_EPS) * g


@functools.lru_cache(maxsize=None)
def _chunk_consts():
    C = CHUNK
    t = np.arange(C)[:, None]
    i = np.arange(C)[None, :]
    tri = (i <= t).astype(np.float32)
    rem = (i > t).astype(np.float32)
    mats = [tri, rem]
    masks = []
    m = C // 2
    while m >= HG_BASE:
        r = (t // (2 * m)) * (2 * m) + m - 1
        mats.append(((i > r) & (i <= t)).astype(np.float32) - ((i > t) & (i <= r)).astype(np.float32))
        same = (t // (2 * m)) == (i // (2 * m))
        masks.append((same & ((t % (2 * m)) >= m) & ((i % (2 * m)) < m)).astype(np.float32))
        m //= 2
    r0 = (t // HG_BASE) * HG_BASE
    mats.append(((i > r0) & (i <= t)).astype(np.float32))
    masks.append((((t // HG_BASE) == (i // HG_BASE)) & (i <= t)).astype(np.float32))
    dall = np.concatenate(mats, 0)
    hmask = np.stack(masks, 0)
    trirem = np.concatenate([tri, rem], 0)
    rmask = np.stack([(i < t).astype(np.float32), (i <= t).astype(np.float32)], 0)
    ones_bd = np.kron(np.eye(HEADS, dtype=np.float32), np.ones((HD, HD), np.float32))
    return dall, hmask, trirem, rmask, ones_bd


N_LEVELS = 4


def _dense_in_kernel(x_ref, g_ref, w_ref, z_ref):
    h = _rmsnorm(x_ref[...], g_ref[...])
    z_ref[...] = jnp.dot(h.astype(BF16), w_ref[...], preferred_element_type=F32)


def _dense_in(x, g, w):
    rows = x.shape[0]
    tile = min(ROW_TILE, rows)
    return pl.pallas_call(
        _dense_in_kernel,
        out_shape=jax.ShapeDtypeStruct((rows, D_IN), F32),
        grid=(rows // tile,),
        in_specs=[
            pl.BlockSpec((tile, D_MODEL), lambda i: (i, 0)),
            pl.BlockSpec((1, D_MODEL), lambda i: (0, 0)),
            pl.BlockSpec((D_MODEL, D_IN), lambda i: (0, 0)),
        ],
        out_specs=pl.BlockSpec((tile, D_IN), lambda i: (i, 0)),
        compiler_params=pltpu.CompilerParams(
            dimension_semantics=("arbitrary",), vmem_limit_bytes=TPU_V7X_VMEM_LIMIT),
        name="dense_in",
    )(x, g, w)


def _dense_out_kernel(x_ref, o_ref, p_ref, wo_ref, gg_ref, wg_ref, wp_ref, gf_ref, y_ref, *, final):
    x1 = x_ref[...] + jnp.dot(o_ref[...], wo_ref[...], preferred_element_type=F32)
    gate = _sigmoid(jnp.dot(_rmsnorm(x1, gg_ref[...]).astype(BF16), wg_ref[...], preferred_element_type=F32))
    x2 = x1 + jnp.dot(p_ref[...].astype(BF16), wp_ref[...], preferred_element_type=F32) * gate
    if final:
        x2 = _rmsnorm(x2, gf_ref[...])
    y_ref[...] = x2


def _dense_out(x, o, p, wo, gg, wg, wp, gf, final):
    rows = x.shape[0]
    tile = min(ROW_TILE, rows)
    row_spec = lambda n: pl.BlockSpec((tile, n), lambda i: (i, 0))
    full = lambda a: pl.BlockSpec(a.shape, lambda i: (0, 0))
    return pl.pallas_call(
        functools.partial(_dense_out_kernel, final=final),
        out_shape=jax.ShapeDtypeStruct((rows, D_MODEL), F32),
        grid=(rows // tile,),
        in_specs=[row_spec(D_MODEL), row_spec(D_MIX), row_spec(D_PLE), full(wo), full(gg), full(wg), full(wp), full(gf)],
        out_specs=row_spec(D_MODEL),
        compiler_params=pltpu.CompilerParams(
            dimension_semantics=("arbitrary",), vmem_limit_bytes=TPU_V7X_VMEM_LIMIT),
        name="dense_out",
    )(x, o, p, wo, gg, wg, wp, gf)


def _s5_scan(bu_ref, sall_ref, lre, lim, s_re, s_im, nt, nb):
    lr = jnp.broadcast_to(lre, (nb, D_S5_STATE))
    li = jnp.broadcast_to(lim, (nb, D_S5_STATE))

    def step(t, carry):
        sr, si = carry
        r0 = pl.multiple_of(t * nb, nb)
        br = bu_ref[pl.ds(r0, nb), 0:D_S5_STATE]
        bi = bu_ref[pl.ds(r0, nb), D_S5_STATE:2 * D_S5_STATE]
        nr = lr * sr - li * si + br
        ni = lr * si + li * sr + bi
        sall_ref[pl.ds(r0, nb), 0:D_S5_STATE] = nr
        sall_ref[pl.ds(r0, nb), D_S5_STATE:2 * D_S5_STATE] = ni
        return nr, ni

    return lax.fori_loop(0, nt, step, (s_re, s_im))


def _s5_head(u, sall_ref, cblk_ref, d_ref, g1_ref, g2_ref):
    y = jnp.dot(sall_ref[...].astype(BF16), cblk_ref[...], preferred_element_type=F32) + d_ref[...] * u
    y = jax.nn.gelu(y)
    yb = y.astype(BF16)
    return jnp.dot(yb, g1_ref[...], preferred_element_type=F32) * _sigmoid(
        jnp.dot(yb, g2_ref[...], preferred_element_type=F32))


def _mix_prompt_kernel(z_ref, lb_ref, hgn_ref, mu_ref, w0_ref, a0_ref, kk_ref, ka_ref, rk_ref, gng_ref, gnb_ref,
                       lre_ref, lim_ref, d_ref, wup_ref, aup_ref, bblk_ref, cblk_ref, g1_ref, g2_ref,
                       dall_ref, hmask_ref, trirem_ref, rmask_ref, ones_ref,
                       o_ref, hg_ref, rw_ref, sh_ref, sre_ref, sim_ref,
                       ocat, ycat, upad, up, bu, sall, ypad):
    nb = z_ref.shape[0]
    C = CHUNK

    @pl.when(pl.program_id(0) == 0)
    def _():
        hg_ref[...] = jnp.zeros(hg_ref.shape, F32)
        rw_ref[...] = jnp.zeros(rw_ref.shape, F32)
        sh_ref[...] = jnp.zeros(sh_ref.shape, F32)
        sre_ref[...] = jnp.zeros(sre_ref.shape, F32)
        sim_ref[...] = jnp.zeros(sim_ref.shape, F32)

    ones_bd = ones_ref[...]
    row_is0 = lax.broadcasted_iota(jnp.int32, (C, D_SHIFT), 0) == 0

    def per_batch(b, carry):
        q = z_ref[b, :, C_Q:C_Q + D_HEADS]
        f = z_ref[b, :, C_F:C_F + D_HEADS]
        iv = z_ref[b, :, C_I:C_I + D_HEADS]
        lbs = jnp.maximum(lb_ref[...], LB_FLOOR)
        fg = lbs + (1.0 - lbs) * _sigmoid(f)
        logf = jnp.log(fg)
        kf = (1.0 - lbs) * _sigmoid(-f)
        qf = _silu(q)
        G = _sel_left(dall_ref[...], logf)
        cum = G[0:C]
        qin = (qf * jnp.exp(cum)).astype(BF16)
        kend = (kf * jnp.exp(G[C:2 * C])).astype(BF16)
        qe, ke = [], []
        for l in range(N_LEVELS):
            e = jnp.exp(-jnp.abs(G[(2 + l) * C:(3 + l) * C]))
            qe.append((qf * e).astype(BF16))
            ke.append((kf * e).astype(BF16))
        dq = G[(2 + N_LEVELS) * C:(3 + N_LEVELS) * C]
        qe.append((qf * jnp.exp(dq)).astype(BF16))
        ke.append((kf * jnp.exp(-dq)).astype(BF16))
        gcol = jnp.exp(cum[C - 8:C, :].T)
        ib = iv.astype(BF16)
        for h in range(HEADS):
            sl = slice(h * HD, (h + 1) * HD)
            att = hmask_ref[0] * _dot_nt(qe[0][:, sl], ke[0][:, sl])
            for l in range(1, N_LEVELS + 1):
                att = att + hmask_ref[l] * _dot_nt(qe[l][:, sl], ke[l][:, sl])
            s0 = hg_ref[b, h]
            ocat[:, sl] = _dot(att, ib[:, sl]) + _dot(qin[:, sl], s0)
            hg_ref[b, h] = gcol[sl, 7:8] * s0 + _dot_tn(kend[:, sl], ib[:, sl])
        o = ocat[...]
        ssq = _sel_right(o * o, ones_bd)
        o = o * lax.rsqrt(ssq * (1.0 / HD) + NORM_EPS) * hgn_ref[...]
        o_ref[b, :, 0:D_HEADS] = (o * _silu(z_ref[b, :, C_GA:C_GA + D_HEADS])).astype(BF16)

        c = z_ref[b, :, C_RW:C_RW + D_SHIFT]
        prev = jnp.where(row_is0, sh_ref[pl.ds(b, 1), :], pltpu.roll(c, 1, axis=0))
        sh_ref[pl.ds(b, 1), :] = c[C - 1:C, :]
        cs = c + mu_ref[...] * (prev - c)
        r = cs[:, 0:D_HEADS]
        k = cs[:, D_HEADS:2 * D_HEADS]
        v = cs[:, 2 * D_HEADS:3 * D_HEADS]
        wd = cs[:, 3 * D_HEADS:3 * D_HEADS + LORA]
        ad = cs[:, 3 * D_HEADS + LORA:D_SHIFT]
        logw = -RW_DECAY_SCALE * _sigmoid(w0_ref[...] + _dot(jnp.tanh(wd), wup_ref[...]))
        a = _sigmoid(a0_ref[...] + _dot(ad, aup_ref[...]))
        kk = k * kk_ref[...]
        kk = kk * lax.rsqrt(jnp.maximum(_sel_right(kk * kk, ones_bd), 1e-24))
        kt = k * (1.0 + (a - 1.0) * ka_ref[...])
        beta = kk * a
        bonus = _sel_right(r * kt * rk_ref[...], ones_bd)
        CL = _sel_left(trirem_ref[...], logw)
        cl = CL[0:C]
        aq = kk * jnp.exp(cl - logw)
        rq = r * jnp.exp(cl)
        ecl = jnp.exp(-cl)
        kd = (kt * ecl).astype(BF16)
        bd = (beta * ecl).astype(BF16)
        eend = jnp.exp(CL[C:2 * C])
        ktl = kt * eend
        btl = beta * eend
        gam = jnp.exp(cl[C - 1:C, :])
        strict = rmask_ref[0]
        incl = rmask_ref[1]
        for h in range(HEADS):
            sl = slice(h * HD, (h + 1) * HD)
            aqh = aq[:, sl]
            rqh = rq[:, sl]
            ar = jnp.concatenate([aqh, rqh], axis=0).astype(BF16)
            zb = _dot_nt(ar, bd[:, sl])
            zk = _dot_nt(ar, kd[:, sl])
            mb = zb[0:C] * strict
            nbm = zb[C:2 * C] * incl
            mk = zk[0:C] * strict
            nk = zk[C:2 * C] * incl
            vh = v[:, sl]
            vb = vh.astype(BF16)
            x = jnp.concatenate([aqh, _dot(mk, vb)], axis=1)
            pw = [mb.astype(BF16)]
            n = 1
            while 2 * n < C:
                pw.append(_dot(pw[-1], pw[-1]).astype(BF16))
                n *= 2
            for p in reversed(pw[1:]):
                x = x + _dot(p, x)
            x = x - _dot(pw[0], x)
            x1 = x[:, 0:HD]
            x2 = x[:, HD:2 * HD]
            w = _dot_tn(btl[:, sl], x1)
            gt = _dot_tn(jnp.concatenate([vh, x2], axis=0), jnp.concatenate([ktl[:, sl], -btl[:, sl]], axis=0))
            nx = _dot(nbm, x)
            s0 = rw_ref[b, h]
            ycat[:, sl] = _dot_nt(rqh - nx[:, 0:HD], s0) + _dot(nk, vb) - nx[:, HD:2 * HD]
            rw_ref[b, h] = s0 * gam[:, sl] - _dot_nt(s0, w) + gt
        y = ycat[...]
        yc = y - _sel_right(y, ones_bd) * (1.0 / HD)
        var = _sel_right(yc * yc, ones_bd) * (1.0 / HD)
        yn = yc * lax.rsqrt(var + RW_GN_EPS) * gng_ref[...] + gnb_ref[...] + bonus * v
        o_ref[b, :, D_HEADS:2 * D_HEADS] = (yn * _silu(z_ref[b, :, C_GB:C_GB + D_HEADS])).astype(BF16)

        for j in range(D_S5 // 128):
            upad[j, pl.ds(pl.multiple_of(b * PITCH, 8), C), :] = z_ref[b, :, C_U + j * 128:C_U + (j + 1) * 128]
        return carry

    lax.fori_loop(0, nb, per_batch, 0)

    for t in range(C):
        for j in range(D_S5 // 128):
            up[t * nb:(t + 1) * nb, j * 128:(j + 1) * 128] = upad[j, pl.ds(t, nb, stride=PITCH), :]
    u = up[...]
    bu[...] = jnp.dot(u.astype(BF16), bblk_ref[...], preferred_element_type=F32)
    s_re, s_im = _s5_scan(bu, sall, lre_ref[...], lim_ref[...], sre_ref[...], sim_ref[...], C, nb)
    sre_ref[...] = s_re
    sim_ref[...] = s_im
    yg = _s5_head(u, sall, cblk_ref, d_ref, g1_ref, g2_ref)
    for t in range(C):
        for j in range(D_S5 // 128):
            ypad[j, pl.ds(t, nb, stride=PITCH), :] = yg[t * nb:(t + 1) * nb, j * 128:(j + 1) * 128]
    for b in range(nb):
        for j in range(D_S5 // 128):
            gate = _silu(z_ref[b, :, C_GC + j * 128:C_GC + (j + 1) * 128])
            o_ref[b, :, 2 * D_HEADS + j * 128:2 * D_HEADS + (j + 1) * 128] = (
                ypad[j, b * PITCH:b * PITCH + C, :] * gate).astype(BF16)


def _mix_prompt(z, vecs, mats, consts):
    B, L, _ = z.shape
    nt = L // CHUNK
    full = lambda a: pl.BlockSpec(a.shape, lambda i, _n=None, _a=a: (0,) * _a.ndim)
    out_shape = (
        jax.ShapeDtypeStruct((B, L, D_MIX), BF16),
        jax.ShapeDtypeStruct((B, HEADS, HD, HD), F32),
        jax.ShapeDtypeStruct((B, HEADS, HD, HD), F32),
        jax.ShapeDtypeStruct((B, D_SHIFT), F32),
        jax.ShapeDtypeStruct((B, D_S5_STATE), F32),
        jax.ShapeDtypeStruct((B, D_S5_STATE), F32),
    )
    out_specs = (
        pl.BlockSpec((B, CHUNK, D_MIX), lambda i: (0, i, 0)),
        pl.BlockSpec((B, HEADS, HD, HD), lambda i: (0, 0, 0, 0)),
        pl.BlockSpec((B, HEADS, HD, HD), lambda i: (0, 0, 0, 0)),
        pl.BlockSpec((B, D_SHIFT), lambda i: (0, 0)),
        pl.BlockSpec((B, D_S5_STATE), lambda i: (0, 0)),
        pl.BlockSpec((B, D_S5_STATE), lambda i: (0, 0)),
    )
    args = (z,) + tuple(vecs) + tuple(mats) + tuple(consts)
    in_specs = [pl.BlockSpec((B, CHUNK, D_IN), lambda i: (0, i, 0))] + [full(a) for a in args[1:]]
    rows = B * CHUNK
    scratch = [
        pltpu.VMEM((CHUNK, D_HEADS), F32),
        pltpu.VMEM((CHUNK, D_HEADS), F32),
        pltpu.VMEM((D_S5 // 128, B * PITCH, 128), F32),
        pltpu.VMEM((rows, D_S5), F32),
        pltpu.VMEM((rows, 2 * D_S5_STATE), F32),
        pltpu.VMEM((rows, 2 * D_S5_STATE), F32),
        pltpu.VMEM((D_S5 // 128, B * PITCH, 128), F32),
    ]
    return pl.pallas_call(
        _mix_prompt_kernel,
        out_shape=out_shape,
        grid=(nt,),
        in_specs=in_specs,
        out_specs=out_specs,
        scratch_shapes=scratch,
        compiler_params=pltpu.CompilerParams(
            dimension_semantics=("arbitrary",), vmem_limit_bytes=TPU_V7X_VMEM_LIMIT),
        name="mix_prompt",
    )(*args)


SAMPLE_BLOCK = 8


def _mix_sample_kernel(z_ref, hg_in, rw_in, sh_in, lb_ref, hgn_ref, mu_ref, w0_ref, a0_ref, kk_ref, ka_ref, rk_ref,
                       gng_ref, gnb_ref, wup_ref, aup_ref, ones_ref,
                       o_ref, hg_ref, rw_ref, sh_ref,
                       rows, osc, ysc, tmp_a, tmp_b, ytr, orow):
    T, nb, _ = z_ref.shape
    ones_bd = ones_ref[...]
    lbs = jnp.maximum(lb_ref[...], LB_FLOOR)
    R_FG, R_KF, R_QF, R_I, R_KK, R_DEC, R_BETA, R_KT, R_R, R_V = range(10)
    for t in range(T):
        q = z_ref[t, :, C_Q:C_Q + D_HEADS]
        f = z_ref[t, :, C_F:C_F + D_HEADS]
        rows[t, R_FG] = lbs + (1.0 - lbs) * _sigmoid(f)
        rows[t, R_KF] = (1.0 - lbs) * _sigmoid(-f)
        rows[t, R_QF] = _silu(q)
        rows[t, R_I] = z_ref[t, :, C_I:C_I + D_HEADS]
        c = z_ref[t, :, C_RW:C_RW + D_SHIFT]
        prev = sh_in[...] if t == 0 else z_ref[t - 1, :, C_RW:C_RW + D_SHIFT]
        cs = c + mu_ref[...] * (prev - c)
        r = cs[:, 0:D_HEADS]
        k = cs[:, D_HEADS:2 * D_HEADS]
        v = cs[:, 2 * D_HEADS:3 * D_HEADS]
        wd = cs[:, 3 * D_HEADS:3 * D_HEADS + LORA]
        ad = cs[:, 3 * D_HEADS + LORA:D_SHIFT]
        logw = -RW_DECAY_SCALE * _sigmoid(w0_ref[...] + _dot(jnp.tanh(wd), wup_ref[...]))
        a = _sigmoid(a0_ref[...] + _dot(ad, aup_ref[...]))
        kk = k * kk_ref[...]
        kk = kk * lax.rsqrt(jnp.maximum(_sel_right(kk * kk, ones_bd), 1e-24))
        kt = k * (1.0 + (a - 1.0) * ka_ref[...])
        rows[t, R_KK] = kk
        rows[t, R_DEC] = jnp.exp(logw)
        rows[t, R_BETA] = kk * a
        rows[t, R_KT] = kt
        rows[t, R_R] = r
        rows[t, R_V] = v
    sh_ref[...] = z_ref[T - 1, :, C_RW:C_RW + D_SHIFT]
    tmp_a[...] = jnp.zeros(tmp_a.shape, F32)
    tmp_b[...] = jnp.zeros(tmp_b.shape, F32)
    ytr[...] = jnp.zeros(ytr.shape, F32)

    def per_seq(b, carry):
        row = lambda t, kind: rows[t, kind, pl.ds(b, 1), :]
        for t in range(T):
            tmp_a[t:t + 1, :] = row(t, R_FG)
            tmp_a[T + t:T + t + 1, :] = row(t, R_KF)
            tmp_b[t:t + 1, :] = row(t, R_QF)
            tmp_b[T + t:T + t + 1, :] = row(t, R_V)
        col_a = tmp_a[...].T
        col_b = tmp_b[...].T
        for h in range(HEADS):
            sl = slice(h * HD, (h + 1) * HD)
            s = hg_in[b, h]
            for t in range(T):
                s = col_a[sl, t:t + 1] * s + col_a[sl, T + t:T + t + 1] * row(t, R_I)[:, sl]
                orow[t:t + 1, sl] = jnp.sum(col_b[sl, t:t + 1] * s, axis=0, keepdims=True)
            hg_ref[b, h] = s
            s = rw_in[b, h]
            for t in range(T):
                sa = jnp.sum(s * row(t, R_KK)[:, sl], axis=1, keepdims=True)
                s = (s * row(t, R_DEC)[:, sl] - sa * row(t, R_BETA)[:, sl]
                     + col_b[sl, T + t:T + t + 1] * row(t, R_KT)[:, sl])
                ytr[sl, t:t + 1] = jnp.sum(s * row(t, R_R)[:, sl], axis=1, keepdims=True)
            rw_ref[b, h] = s
        yrow = ytr[...].T
        for t in range(T):
            ysc[t, pl.ds(b, 1), :] = yrow[t:t + 1, :]
            osc[t, pl.ds(b, 1), :] = orow[t:t + 1, :]
        return carry

    lax.fori_loop(0, nb, per_seq, 0)

    for t in range(T):
        o = osc[t]
        ssq = _sel_right(o * o, ones_bd)
        o = o * lax.rsqrt(ssq * (1.0 / HD) + NORM_EPS) * hgn_ref[...]
        o_ref[t, :, 0:D_HEADS] = (o * _silu(z_ref[t, :, C_GA:C_GA + D_HEADS])).astype(BF16)
        y = ysc[t]
        yc = y - _sel_right(y, ones_bd) * (1.0 / HD)
        var = _sel_right(yc * yc, ones_bd) * (1.0 / HD)
        bonus = _sel_right(rows[t, R_R] * rows[t, R_KT] * rk_ref[...], ones_bd)
        yn = yc * lax.rsqrt(var + RW_GN_EPS) * gng_ref[...] + gnb_ref[...] + bonus * rows[t, R_V]
        o_ref[t, :, D_HEADS:2 * D_HEADS] = (yn * _silu(z_ref[t, :, C_GB:C_GB + D_HEADS])).astype(BF16)


def _mix_sample(z, hg, rw, sh, vecs, wup, aup, ones_bd):
    T, B, _ = z.shape
    nb = SAMPLE_BLOCK
    full = lambda a: pl.BlockSpec(a.shape, lambda i, _a=a: (0,) * _a.ndim)
    st4 = pl.BlockSpec((nb, HEADS, HD, HD), lambda i: (i, 0, 0, 0))
    sh_spec = pl.BlockSpec((nb, D_SHIFT), lambda i: (i, 0))
    rest = tuple(vecs) + (wup, aup, ones_bd)
    return pl.pallas_call(
        _mix_sample_kernel,
        out_shape=(
            jax.ShapeDtypeStruct((T, B, 2 * D_HEADS), BF16),
            jax.ShapeDtypeStruct((B, HEADS, HD, HD), F32),
            jax.ShapeDtypeStruct((B, HEADS, HD, HD), F32),
            jax.ShapeDtypeStruct((B, D_SHIFT), F32),
        ),
        grid=(B // nb,),
        in_specs=[pl.BlockSpec((T, nb, D_IN), lambda i: (0, i, 0)), st4, st4, sh_spec] + [full(a) for a in rest],
        out_specs=(pl.BlockSpec((T, nb, 2 * D_HEADS), lambda i: (0, i, 0)), st4, st4, sh_spec),
        scratch_shapes=[
            pltpu.VMEM((T, 10, nb, D_HEADS), F32),
            pltpu.VMEM((T, nb, D_HEADS), F32),
            pltpu.VMEM((T, nb, D_HEADS), F32),
            pltpu.VMEM((8, D_HEADS), F32),
            pltpu.VMEM((8, D_HEADS), F32),
            pltpu.VMEM((D_HEADS, 8), F32),
            pltpu.VMEM((8, D_HEADS), F32),
        ],
        compiler_params=pltpu.CompilerParams(
            dimension_semantics=("arbitrary",), vmem_limit_bytes=TPU_V7X_VMEM_LIMIT),
        name="mix_sample",
    )(z, hg, rw, sh, *rest)


def _s5_sample_kernel(z_ref, sre_in, sim_in, lre_ref, lim_ref, d_ref, bblk_ref, cblk_ref, g1_ref, g2_ref,
                      o_ref, sre_ref, sim_ref, bu, sall, *, nt, nb):
    u = z_ref[:, C_U:C_U + D_S5]
    bu[...] = jnp.dot(u.astype(BF16), bblk_ref[...], preferred_element_type=F32)
    s_re, s_im = _s5_scan(bu, sall, lre_ref[...], lim_ref[...], sre_in[...], sim_in[...], nt, nb)
    sre_ref[...] = s_re
    sim_ref[...] = s_im
    yg = _s5_head(u, sall, cblk_ref, d_ref, g1_ref, g2_ref)
    o_ref[...] = (yg * _silu(z_ref[:, C_GC:C_GC + D_S5])).astype(BF16)


def _s5_sample(z2d, sre, sim, lre, lim, d, bblk, cblk, g1, g2, nt, nb):
    rows = z2d.shape[0]
    return pl.pallas_call(
        functools.partial(_s5_sample_kernel, nt=nt, nb=nb),
        out_shape=(
            jax.ShapeDtypeStruct((rows, D_S5), BF16),
            jax.ShapeDtypeStruct(sre.shape, F32),
            jax.ShapeDtypeStruct(sim.shape, F32),
        ),
        scratch_shapes=[pltpu.VMEM((rows, 2 * D_S5_STATE), F32), pltpu.VMEM((rows, 2 * D_S5_STATE), F32)],
        compiler_params=pltpu.CompilerParams(vmem_limit_bytes=TPU_V7X_VMEM_LIMIT),
        name="s5_sample",
    )(z2d, sre, sim, lre, lim, d, bblk, cblk, g1, g2)


def _prep_kernel(lbraw_ref, are_ref, aim_ref, ldt_ref, bre_ref, bim_ref, lb_ref, lre_ref, lim_ref, bbre_ref, bbim_ref):
    raw = lbraw_ref[...]
    e = jnp.exp(raw - jnp.max(raw, axis=0, keepdims=True))
    sm = e / jnp.sum(e, axis=0, keepdims=True)
    depth = raw.shape[0]
    acc = jnp.zeros_like(sm[0:1])
    for l in range(depth):
        acc = acc + sm[l:l + 1]
        lb_ref[l:l + 1, :] = acc - sm[0:1]
    a_re = are_ref[...]
    a_im = aim_ref[...]
    dt = jnp.exp(ldt_ref[...])
    mag = jnp.exp(dt * a_re)
    lam_re = mag * jnp.cos(dt * a_im)
    lam_im = mag * jnp.sin(dt * a_im)
    den = a_re * a_re + a_im * a_im
    xr = lam_re - 1.0
    f_re = (xr * a_re + lam_im * a_im) / den
    f_im = (lam_im * a_re - xr * a_im) / den
    lre_ref[...] = lam_re
    lim_ref[...] = lam_im
    b_re = bre_ref[...]
    b_im = bim_ref[...]
    bbre_ref[...] = f_re * b_re - f_im * b_im
    bbim_ref[...] = f_re * b_im + f_im * b_re


def _prep(hg_lb_raw, s5_a_re, s5_a_im, s5_log_dt, s5_b_re, s5_b_im):
    depth = hg_lb_raw.shape[0]
    n = depth * S5_GROUPS * S5_P
    col = lambda a: a.reshape(n, 1)
    ldt = jnp.broadcast_to(s5_log_dt[:, :, None], (depth, S5_GROUPS, S5_P))
    lb, lre, lim, bbre, bbim = pl.pallas_call(
        _prep_kernel,
        out_shape=(
            jax.ShapeDtypeStruct((depth, D_HEADS), F32),
            jax.ShapeDtypeStruct((n, 1), F32),
            jax.ShapeDtypeStruct((n, 1), F32),
            jax.ShapeDtypeStruct((n, S5_CH), F32),
            jax.ShapeDtypeStruct((n, S5_CH), F32),
        ),
        name="param_prep",
    )(hg_lb_raw, col(s5_a_re), col(s5_a_im), col(ldt), s5_b_re.reshape(n, S5_CH), s5_b_im.reshape(n, S5_CH))
    shape3 = (depth, S5_GROUPS, S5_P)
    return (lb, lre.reshape(depth, 1, D_S5_STATE), lim.reshape(depth, 1, D_S5_STATE),
            bbre.reshape(shape3 + (S5_CH,)), bbim.reshape(shape3 + (S5_CH,)))


def _block_diag_in(bb):
    eye = jnp.eye(S5_GROUPS, dtype=F32)
    return jnp.einsum('gpc,gh->gchp', bb, eye).reshape(D_S5, D_S5_STATE)


def _block_diag_out(c):
    eye = jnp.eye(S5_GROUPS, dtype=F32)
    return jnp.einsum('gcp,gh->gphc', c, eye).reshape(D_S5_STATE, D_S5)


def kernel(x_prompt, x_sample, state_hgrn, state_rwkv, state_rwkv_shift, state_s5_re, state_s5_im, p_prompt, p_sample, g_in, w_in, hg_lb_raw, hg_norm_g, rw_mu, rw_w0, rw_w_up, rw_a0, rw_a_up, rw_k_k, rw_k_a, rw_r_k, rw_gn_g, rw_gn_b, s5_a_re, s5_a_im, s5_log_dt, s5_b_re, s5_b_im, s5_c_re, s5_c_im, s5_d, s5_glu_w1, s5_glu_w2, w_out, ple_w_proj, ple_gate_g, ple_w_gate, g_final):
    depth = w_in.shape[0]
    B, L, _ = x_prompt.shape
    BS, T, _ = x_sample.shape
    dall, hmask, trirem, rmask, ones_np = _chunk_consts()
    consts = (jnp.asarray(dall, BF16), jnp.asarray(hmask, F32), jnp.asarray(trirem, BF16),
              jnp.asarray(rmask, F32), jnp.asarray(ones_np, BF16))
    ones_bd = consts[4]

    lb_all, lam_re, lam_im, bb_re, bb_im = _prep(hg_lb_raw, s5_a_re, s5_a_im, s5_log_dt, s5_b_re, s5_b_im)
    row = lambda a: a.reshape(1, -1)
    g_final2 = row(g_final)

    xp = x_prompt.reshape(B * L, D_MODEL)
    xs = jnp.transpose(x_sample, (1, 0, 2)).reshape(T * BS, D_MODEL)
    pp = p_prompt.reshape(depth, B * L, D_PLE)
    ps = jnp.transpose(p_sample, (0, 2, 1, 3)).reshape(depth, T * BS, D_PLE)

    outs_p = [[] for _ in range(5)]
    outs_s = [[] for _ in range(5)]
    for l in range(depth):
        w_in_l = w_in[l].astype(BF16)
        wo = w_out[l].astype(BF16)
        wg = ple_w_gate[l].astype(BF16)
        wp = ple_w_proj[l].astype(BF16)
        wup = rw_w_up[l].astype(BF16)
        aup = rw_a_up[l].astype(BF16)
        bblk = jnp.concatenate([_block_diag_in(bb_re[l]), _block_diag_in(bb_im[l])], axis=1).astype(BF16)
        cblk = jnp.concatenate([_block_diag_out(s5_c_re[l]), -_block_diag_out(s5_c_im[l])], axis=0).astype(BF16)
        g1 = s5_glu_w1[l].astype(BF16)
        g2 = s5_glu_w2[l].astype(BF16)
        hr_vecs = (row(lb_all[l]), row(hg_norm_g[l]), row(rw_mu[l]), row(rw_w0[l]), row(rw_a0[l]), row(rw_k_k[l]),
                   row(rw_k_a[l]), row(rw_r_k[l]), row(rw_gn_g[l]), row(rw_gn_b[l]))
        s5_vecs = (lam_re[l], lam_im[l], row(s5_d[l]))
        final = l == depth - 1

        z = _dense_in(xp, row(g_in[l]), w_in_l)
        o, hg, rw, sh, sre, sim = _mix_prompt(
            z.reshape(B, L, D_IN), hr_vecs + s5_vecs, (wup, aup, bblk, cblk, g1, g2), consts)
        xp = _dense_out(xp, o.reshape(B * L, D_MIX), pp[l], wo, row(ple_gate_g[l]), wg, wp, g_final2, final)
        for acc, val in zip(outs_p, (hg, rw, sh, sre.reshape(B, S5_GROUPS, S5_P), sim.reshape(B, S5_GROUPS, S5_P))):
            acc.append(val)

        zs = _dense_in(xs, row(g_in[l]), w_in_l)
        o_ab, hg, rw, sh = _mix_sample(zs.reshape(T, BS, D_IN), state_hgrn[l], state_rwkv[l], state_rwkv_shift[l],
                                       hr_vecs, wup, aup, ones_bd)
        o_c, sre, sim = _s5_sample(zs, state_s5_re[l].reshape(BS, D_S5_STATE), state_s5_im[l].reshape(BS, D_S5_STATE),
                                   lam_re[l], lam_im[l], row(s5_d[l]), bblk, cblk, g1, g2, T, BS)
        os_ = jnp.concatenate([o_ab.reshape(T * BS, 2 * D_HEADS), o_c], axis=1)
        xs = _dense_out(xs, os_, ps[l], wo, row(ple_gate_g[l]), wg, wp, g_final2, final)
        for acc, val in zip(outs_s, (hg, rw, sh, sre.reshape(BS, S5_GROUPS, S5_P), sim.reshape(BS, S5_GROUPS, S5_P))):
            acc.append(val)

    y_prompt = xp.reshape(B, L, D_MODEL)
    y_sample = jnp.transpose(xs.reshape(T, BS, D_MODEL), (1, 0, 2))
    return (y_prompt, y_sample) + tuple(jnp.stack(a) for a in outs_p) + tuple(jnp.stack(a) for a in outs_s)
```

```python
import functools
import math

import jax
import jax.numpy as jnp
import numpy as np
from jax import lax
from jax.experimental import pallas as pl
from jax.experimental.pallas import tpu as pltpu

F32 = jnp.float32
BF16 = jnp.bfloat16

D_MODEL = 1024
D_PLE = 256
HEADS = 6
HD = 64
D_HEADS = HEADS * HD
LORA = 64
D_SHIFT = 3 * D_HEADS + 2 * LORA
S5_GROUPS = 16
S5_CH = 16
S5_P = 64
D_S5 = S5_GROUPS * S5_CH
D_S5_STATE = S5_GROUPS * S5_P
D_IN = 4 * D_HEADS + D_SHIFT + D_HEADS + 2 * D_S5
D_MIX = 2 * D_HEADS + D_S5

C_Q, C_F, C_I, C_GA = 0, D_HEADS, 2 * D_HEADS, 3 * D_HEADS
C_RW = 4 * D_HEADS
C_GB = C_RW + D_SHIFT
C_U = C_GB + D_HEADS
C_GC = C_U + D_S5

LB_FLOOR = 1e-12
NORM_EPS = 1e-6
RW_GN_EPS = 64e-5
RW_DECAY_SCALE = math.exp(-0.5)

CHUNK = 64
HG_BASE = 4
BATCH_UNROLL = 1
ROW_TILE = 512
PITCH = CHUNK + 8
TPU_V7X_VMEM_LIMIT = 56 * 1024 * 1024


def _dot(a, b):
    return jnp.dot(a.astype(BF16), b.astype(BF16), preferred_element_type=F32)


def _dot_nt(a, b):
    return lax.dot_general(a.astype(BF16), b.astype(BF16), (((1,), (1,)), ((), ())), preferred_element_type=F32)


def _dot_tn(a, b):
    return lax.dot_general(a.astype(BF16), b.astype(BF16), (((0,), (0,)), ((), ())), preferred_element_type=F32)


def _split3(x):
    hi = x.astype(BF16)
    r1 = x - hi.astype(F32)
    mid = r1.astype(BF16)
    lo = (r1 - mid.astype(F32)).astype(BF16)
    return hi, mid, lo


def _sel_left(m, x):
    hi, mid, lo = _split3(x)
    d = lambda p: jnp.dot(m, p, preferred_element_type=F32)
    return d(hi) + d(mid) + d(lo)


def _sel_right(x, m):
    hi = x.astype(BF16)
    lo = (x - hi.astype(F32)).astype(BF16)
    d = lambda p: jnp.dot(p, m, preferred_element_type=F32)
    return d(hi) + d(lo)


def _sigmoid(x):
    return jax.nn.sigmoid(x)


def _silu(x):
    return x * jax.nn.sigmoid(x)


def _rmsnorm(x, g):
    return x * lax.rsqrt(jnp.mean(x * x, axis=-1, keepdims=True) + NORM_EPS) * g


@functools.lru_cache(maxsize=None)
def _chunk_consts():
    C = CHUNK
    t = np.arange(C)[:, None]
    i = np.arange(C)[None, :]
    tri = (i <= t).astype(np.float32)
    rem = (i > t).astype(np.float32)
    mats = [tri, rem]
    masks = []
    m = C // 2
    while m >= HG_BASE:
        r = (t // (2 * m)) * (2 * m) + m - 1
        mats.append(((i > r) & (i <= t)).astype(np.float32) - ((i > t) & (i <= r)).astype(np.float32))
        same = (t // (2 * m)) == (i // (2 * m))
        masks.append((same & ((t % (2 * m)) >= m) & ((i % (2 * m)) < m)).astype(np.float32))
        m //= 2
    r0 = (t // HG_BASE) * HG_BASE
    mats.append(((i > r0) & (i <= t)).astype(np.float32))
    masks.append((((t // HG_BASE) == (i // HG_BASE)) & (i <= t)).astype(np.float32))
    dall = np.concatenate(mats, 0)
    hmask = np.stack(masks, 0)
    trirem = np.concatenate([tri, rem], 0)
    rmask = np.stack([(i < t).astype(np.float32), (i <= t).astype(np.float32)], 0)
    ones_bd = np.kron(np.eye(HEADS, dtype=np.float32), np.ones((HD, HD), np.float32))
    return dall, hmask, trirem, rmask, ones_bd


N_LEVELS = 4


def _dense_in_kernel(x_ref, g_ref, w_ref, z_ref):
    h = _rmsnorm(x_ref[...], g_ref[...])
    z_ref[...] = jnp.dot(h.astype(BF16), w_ref[...], preferred_element_type=F32)


def _dense_in(x, g, w):
    rows = x.shape[0]
    tile = min(ROW_TILE, rows)
    return pl.pallas_call(
        _dense_in_kernel,
        out_shape=jax.ShapeDtypeStruct((rows, D_IN), F32),
        grid=(rows // tile,),
        in_specs=[
            pl.BlockSpec((tile, D_MODEL), lambda i: (i, 0)),
            pl.BlockSpec((1, D_MODEL), lambda i: (0, 0)),
            pl.BlockSpec((D_MODEL, D_IN), lambda i: (0, 0)),
        ],
        out_specs=pl.BlockSpec((tile, D_IN), lambda i: (i, 0)),
        compiler_params=pltpu.CompilerParams(
            dimension_semantics=("arbitrary",), vmem_limit_bytes=TPU_V7X_VMEM_LIMIT),
        name="dense_in",
    )(x, g, w)


def _dense_out_kernel(x_ref, o_ref, p_ref, wo_ref, gg_ref, wg_ref, wp_ref, gf_ref, y_ref, *, final):
    x1 = x_ref[...] + jnp.dot(o_ref[...], wo_ref[...], preferred_element_type=F32)
    gate = _sigmoid(jnp.dot(_rmsnorm(x1, gg_ref[...]).astype(BF16), wg_ref[...], preferred_element_type=F32))
    x2 = x1 + jnp.dot(p_ref[...].astype(BF16), wp_ref[...], preferred_element_type=F32) * gate
    if final:
        x2 = _rmsnorm(x2, gf_ref[...])
    y_ref[...] = x2


def _dense_out(x, o, p, wo, gg, wg, wp, gf, final):
    rows = x.shape[0]
    tile = min(ROW_TILE, rows)
    row_spec = lambda n: pl.BlockSpec((tile, n), lambda i: (i, 0))
    full = lambda a: pl.BlockSpec(a.shape, lambda i: (0, 0))
    return pl.pallas_call(
        functools.partial(_dense_out_kernel, final=final),
        out_shape=jax.ShapeDtypeStruct((rows, D_MODEL), F32),
        grid=(rows // tile,),
        in_specs=[row_spec(D_MODEL), row_spec(D_MIX), row_spec(D_PLE), full(wo), full(gg), full(wg), full(wp), full(gf)],
        out_specs=row_spec(D_MODEL),
        compiler_params=pltpu.CompilerParams(
            dimension_semantics=("arbitrary",), vmem_limit_bytes=TPU_V7X_VMEM_LIMIT),
        name="dense_out",
    )(x, o, p, wo, gg, wg, wp, gf)


def _s5_scan(bu_ref, sall_ref, lre, lim, s_re, s_im, nt, nb):
    lr = jnp.broadcast_to(lre, (nb, D_S5_STATE))
    li = jnp.broadcast_to(lim, (nb, D_S5_STATE))

    def step(t, carry):
        sr, si = carry
        r0 = pl.multiple_of(t * nb, nb)
        br = bu_ref[pl.ds(r0, nb), 0:D_S5_STATE]
        bi = bu_ref[pl.ds(r0, nb), D_S5_STATE:2 * D_S5_STATE]
        nr = lr * sr - li * si + br
        ni = lr * si + li * sr + bi
        sall_ref[pl.ds(r0, nb), 0:D_S5_STATE] = nr
        sall_ref[pl.ds(r0, nb), D_S5_STATE:2 * D_S5_STATE] = ni
        return nr, ni

    return lax.fori_loop(0, nt, step, (s_re, s_im))


def _s5_head(u, sall_ref, cblk_ref, d_ref, g1_ref, g2_ref):
    y = jnp.dot(sall_ref[...].astype(BF16), cblk_ref[...], preferred_element_type=F32) + d_ref[...] * u
    y = jax.nn.gelu(y)
    yb = y.astype(BF16)
    return jnp.dot(yb, g1_ref[...], preferred_element_type=F32) * _sigmoid(
        jnp.dot(yb, g2_ref[...], preferred_element_type=F32))


def _mix_prompt_kernel(z_ref, lb_ref, hgn_ref, mu_ref, w0_ref, a0_ref, kk_ref, ka_ref, rk_ref, gng_ref, gnb_ref,
                       lre_ref, lim_ref, d_ref, wup_ref, aup_ref, bblk_ref, cblk_ref, g1_ref, g2_ref,
                       dall_ref, hmask_ref, trirem_ref, rmask_ref, ones_ref,
                       o_ref, hg_ref, rw_ref, sh_ref, sre_ref, sim_ref,
                       upad, up, bu, sall, ypad):
    nb = z_ref.shape[0]
    C = CHUNK

    @pl.when(pl.program_id(0) == 0)
    def _():
        hg_ref[...] = jnp.zeros(hg_ref.shape, F32)
        rw_ref[...] = jnp.zeros(rw_ref.shape, F32)
        sh_ref[...] = jnp.zeros(sh_ref.shape, F32)
        sre_ref[...] = jnp.zeros(sre_ref.shape, F32)
        sim_ref[...] = jnp.zeros(sim_ref.shape, F32)

    ones_bd = ones_ref[...]
    row_is0 = lax.broadcasted_iota(jnp.int32, (C, D_SHIFT), 0) == 0

    def one_batch(b):
        heads = range(HEADS)
        hsl = [slice(h * HD, (h + 1) * HD) for h in heads]
        s_hg = [hg_ref[b, h] for h in heads]
        s_rw = [rw_ref[b, h] for h in heads]
        sh_row = sh_ref[pl.ds(b, 1), :]
        q = z_ref[b, :, C_Q:C_Q + D_HEADS]
        f = z_ref[b, :, C_F:C_F + D_HEADS]
        iv = z_ref[b, :, C_I:C_I + D_HEADS]
        lbs = jnp.maximum(lb_ref[...], LB_FLOOR)
        fg = lbs + (1.0 - lbs) * _sigmoid(f)
        logf = jnp.log(fg)
        kf = (1.0 - lbs) * _sigmoid(-f)
        qf = _silu(q)
        G = _sel_left(dall_ref[...], logf)
        cum = G[0:C]
        qin = (qf * jnp.exp(cum)).astype(BF16)
        kend = (kf * jnp.exp(G[C:2 * C])).astype(BF16)
        qe, ke = [], []
        for l in range(N_LEVELS):
            e = jnp.exp(-jnp.abs(G[(2 + l) * C:(3 + l) * C]))
            qe.append((qf * e).astype(BF16))
            ke.append((kf * e).astype(BF16))
        dq = G[(2 + N_LEVELS) * C:(3 + N_LEVELS) * C]
        qe.append((qf * jnp.exp(dq)).astype(BF16))
        ke.append((kf * jnp.exp(-dq)).astype(BF16))
        gcol = jnp.exp(cum[C - 8:C, :].T)
        ib = iv.astype(BF16)
        prods = [[_dot_nt(qe[l][:, sl], ke[l][:, sl]) for l in range(N_LEVELS + 1)] for sl in hsl]
        att = []
        for h in heads:
            a_h = hmask_ref[0] * prods[h][0]
            for l in range(1, N_LEVELS + 1):
                a_h = a_h + hmask_ref[l] * prods[h][l]
            att.append(a_h.astype(BF16))
        o = jnp.concatenate(
            [_dot(att[h], ib[:, hsl[h]]) + _dot(qin[:, hsl[h]], s_hg[h]) for h in heads], axis=1)
        new_hg = [gcol[hsl[h], 7:8] * s_hg[h] + _dot_tn(kend[:, hsl[h]], ib[:, hsl[h]]) for h in heads]
        ssq = _sel_right(o * o, ones_bd)
        o = o * lax.rsqrt(ssq * (1.0 / HD) + NORM_EPS) * hgn_ref[...]
        o_a = (o * _silu(z_ref[b, :, C_GA:C_GA + D_HEADS])).astype(BF16)

        c = z_ref[b, :, C_RW:C_RW + D_SHIFT]
        prev = jnp.where(row_is0, sh_row, pltpu.roll(c, 1, axis=0))
        cs = c + mu_ref[...] * (prev - c)
        r = cs[:, 0:D_HEADS]
        k = cs[:, D_HEADS:2 * D_HEADS]
        v = cs[:, 2 * D_HEADS:3 * D_HEADS]
        wd = cs[:, 3 * D_HEADS:3 * D_HEADS + LORA]
        ad = cs[:, 3 * D_HEADS + LORA:D_SHIFT]
        logw = -RW_DECAY_SCALE * _sigmoid(w0_ref[...] + _dot(jnp.tanh(wd), wup_ref[...]))
        a = _sigmoid(a0_ref[...] + _dot(ad, aup_ref[...]))
        kk = k * kk_ref[...]
        kk = kk * lax.rsqrt(jnp.maximum(_sel_right(kk * kk, ones_bd), 1e-24))
        kt = k * (1.0 + (a - 1.0) * ka_ref[...])
        beta = kk * a
        bonus = _sel_right(r * kt * rk_ref[...], ones_bd)
        CL = _sel_left(trirem_ref[...], logw)
        cl = CL[0:C]
        aq = kk * jnp.exp(cl - logw)
        rq = r * jnp.exp(cl)
        ecl = jnp.exp(-cl)
        kd = (kt * ecl).astype(BF16)
        bd = (beta * ecl).astype(BF16)
        eend = jnp.exp(CL[C:2 * C])
        ktl = kt * eend
        btl = beta * eend
        gam = jnp.exp(cl[C - 1:C, :])
        strict = rmask_ref[0]
        incl = rmask_ref[1]
        vb = v.astype(BF16)
        ar = [jnp.concatenate([aq[:, sl], rq[:, sl]], axis=0).astype(BF16) for sl in hsl]
        zb = [_dot_nt(ar[h], bd[:, hsl[h]]) for h in heads]
        zk = [_dot_nt(ar[h], kd[:, hsl[h]]) for h in heads]
        mb = [zb[h][0:C] * strict for h in heads]
        nbm = [(zb[h][C:2 * C] * incl).astype(BF16) for h in heads]
        mk = [(zk[h][0:C] * strict).astype(BF16) for h in heads]
        nk = [(zk[h][C:2 * C] * incl).astype(BF16) for h in heads]
        x0 = [jnp.concatenate([aq[:, hsl[h]], _dot(mk[h], vb[:, hsl[h]])], axis=1) for h in heads]
        pw = [m.astype(BF16) for m in mb]
        bm = [-m for m in mb]
        n = 1
        while 2 * n < C:
            pwf = [_dot(p, p) for p in pw]
            pw = [p.astype(BF16) for p in pwf]
            bm = [bm[h] + pwf[h] + _dot(bm[h], pw[h]) for h in heads]
            n *= 2
        x = [x0[h] + _dot(bm[h], x0[h]) for h in heads]
        w = [_dot_tn(btl[:, hsl[h]], x[h][:, 0:HD]) for h in heads]
        gt = [_dot_tn(jnp.concatenate([v[:, hsl[h]], x[h][:, HD:2 * HD]], axis=0),
                      jnp.concatenate([ktl[:, hsl[h]], -btl[:, hsl[h]]], axis=0)) for h in heads]
        nx = [_dot(nbm[h], x[h]) for h in heads]
        y = jnp.concatenate(
            [_dot_nt(rq[:, hsl[h]] - nx[h][:, 0:HD], s_rw[h]) + _dot(nk[h], vb[:, hsl[h]]) - nx[h][:, HD:2 * HD]
             for h in heads], axis=1)
        new_rw = [s_rw[h] * gam[:, hsl[h]] - _dot_nt(s_rw[h], w[h]) + gt[h] for h in heads]
        yc = y - _sel_right(y, ones_bd) * (1.0 / HD)
        var = _sel_right(yc * yc, ones_bd) * (1.0 / HD)
        yn = yc * lax.rsqrt(var + RW_GN_EPS) * gng_ref[...] + gnb_ref[...] + bonus * v
        o_b = (yn * _silu(z_ref[b, :, C_GB:C_GB + D_HEADS])).astype(BF16)
        return o_a, o_b, new_hg, new_rw, c[C - 1:C, :]

    def batch_group(g, carry):
        bs = [g * BATCH_UNROLL + j for j in range(BATCH_UNROLL)]
        results = [one_batch(b) for b in bs]
        for b, (o_a, o_b, new_hg, new_rw, last_row) in zip(bs, results):
            o_ref[b, :, 0:D_HEADS] = o_a
            o_ref[b, :, D_HEADS:2 * D_HEADS] = o_b
            for h in range(HEADS):
                hg_ref[b, h] = new_hg[h]
                rw_ref[b, h] = new_rw[h]
            sh_ref[pl.ds(b, 1), :] = last_row
            for j in range(D_S5 // 128):
                upad[j, pl.ds(pl.multiple_of(b * PITCH, 8), C), :] = z_ref[b, :, C_U + j * 128:C_U + (j + 1) * 128]
        return carry

    lax.fori_loop(0, nb // BATCH_UNROLL, batch_group, 0)

    for t in range(C):
        for j in range(D_S5 // 128):
            up[t * nb:(t + 1) * nb, j * 128:(j + 1) * 128] = upad[j, pl.ds(t, nb, stride=PITCH), :]
    u = up[...]
    bu[...] = jnp.dot(u.astype(BF16), bblk_ref[...], preferred_element_type=F32)
    s_re, s_im = _s5_scan(bu, sall, lre_ref[...], lim_ref[...], sre_ref[...], sim_ref[...], C, nb)
    sre_ref[...] = s_re
    sim_ref[...] = s_im
    yg = _s5_head(u, sall, cblk_ref, d_ref, g1_ref, g2_ref)
    for t in range(C):
        for j in range(D_S5 // 128):
            ypad[j, pl.ds(t, nb, stride=PITCH), :] = yg[t * nb:(t + 1) * nb, j * 128:(j + 1) * 128]
    for b in range(nb):
        for j in range(D_S5 // 128):
            gate = _silu(z_ref[b, :, C_GC + j * 128:C_GC + (j + 1) * 128])
            o_ref[b, :, 2 * D_HEADS + j * 128:2 * D_HEADS + (j + 1) * 128] = (
                ypad[j, b * PITCH:b * PITCH + C, :] * gate).astype(BF16)


def _mix_prompt(z, vecs, mats, consts):
    B, L, _ = z.shape
    nt = L // CHUNK
    full = lambda a: pl.BlockSpec(a.shape, lambda i, _n=None, _a=a: (0,) * _a.ndim)
    out_shape = (
        jax.ShapeDtypeStruct((B, L, D_MIX), BF16),
        jax.ShapeDtypeStruct((B, HEADS, HD, HD), F32),
        jax.ShapeDtypeStruct((B, HEADS, HD, HD), F32),
        jax.ShapeDtypeStruct((B, D_SHIFT), F32),
        jax.ShapeDtypeStruct((B, D_S5_STATE), F32),
        jax.ShapeDtypeStruct((B, D_S5_STATE), F32),
    )
    out_specs = (
        pl.BlockSpec((B, CHUNK, D_MIX), lambda i: (0, i, 0)),
        pl.BlockSpec((B, HEADS, HD, HD), lambda i: (0, 0, 0, 0)),
        pl.BlockSpec((B, HEADS, HD, HD), lambda i: (0, 0, 0, 0)),
        pl.BlockSpec((B, D_SHIFT), lambda i: (0, 0)),
        pl.BlockSpec((B, D_S5_STATE), lambda i: (0, 0)),
        pl.BlockSpec((B, D_S5_STATE), lambda i: (0, 0)),
    )
    args = (z,) + tuple(vecs) + tuple(mats) + tuple(consts)
    in_specs = [pl.BlockSpec((B, CHUNK, D_IN), lambda i: (0, i, 0))] + [full(a) for a in args[1:]]
    rows = B * CHUNK
    scratch = [
        pltpu.VMEM((D_S5 // 128, B * PITCH, 128), F32),
        pltpu.VMEM((rows, D_S5), F32),
        pltpu.VMEM((rows, 2 * D_S5_STATE), F32),
        pltpu.VMEM((rows, 2 * D_S5_STATE), F32),
        pltpu.VMEM((D_S5 // 128, B * PITCH, 128), F32),
    ]
    return pl.pallas_call(
        _mix_prompt_kernel,
        out_shape=out_shape,
        grid=(nt,),
        in_specs=in_specs,
        out_specs=out_specs,
        scratch_shapes=scratch,
        compiler_params=pltpu.CompilerParams(
            dimension_semantics=("arbitrary",), vmem_limit_bytes=TPU_V7X_VMEM_LIMIT),
        name="mix_prompt",
    )(*args)


SAMPLE_BLOCK = 16


def _head_sums(terms, ones_bd):
    nb = terms[0].shape[0]
    s = _sel_right(jnp.concatenate(terms, axis=0), ones_bd)
    return [s[i * nb:(i + 1) * nb] for i in range(len(terms))]


def _mix_sample_chunk_kernel(z_ref, hg_in, rw_in, sh_in, lb_ref, hgn_ref, mu_ref, w0_ref, a0_ref, kk_ref, ka_ref,
                             rk_ref, gng_ref, gnb_ref, wup_ref, aup_ref, ones_ref,
                             o_ref, hg_ref, rw_ref, sh_ref,
                             lhs_hg, lhs_rw, base_hg, base_rw, upd_hg, upd_rw, gam_hg, gam_rw, tmp):
    T, nb, _ = z_ref.shape
    ones_bd = ones_ref[...]
    lbs = jnp.maximum(lb_ref[...], LB_FLOOR)
    heads = range(HEADS)
    hsl = [slice(h * HD, (h + 1) * HD) for h in heads]

    fg, kf, qf, iv, kk, dec, beta, kt, r, v = ([] for _ in range(10))
    for t in range(T):
        f = z_ref[t, :, C_F:C_F + D_HEADS]
        fg.append(lbs + (1.0 - lbs) * _sigmoid(f))
        kf.append((1.0 - lbs) * _sigmoid(-f))
        qf.append(_silu(z_ref[t, :, C_Q:C_Q + D_HEADS]))
        iv.append(z_ref[t, :, C_I:C_I + D_HEADS])
        c = z_ref[t, :, C_RW:C_RW + D_SHIFT]
        prev = sh_in[...] if t == 0 else z_ref[t - 1, :, C_RW:C_RW + D_SHIFT]
        cs = c + mu_ref[...] * (prev - c)
        k_t = cs[:, D_HEADS:2 * D_HEADS]
        wd = cs[:, 3 * D_HEADS:3 * D_HEADS + LORA]
        ad = cs[:, 3 * D_HEADS + LORA:D_SHIFT]
        logw = -RW_DECAY_SCALE * _sigmoid(w0_ref[...] + _dot(jnp.tanh(wd), wup_ref[...]))
        a = _sigmoid(a0_ref[...] + _dot(ad, aup_ref[...]))
        kk_t = k_t * kk_ref[...]
        kk_t = kk_t * lax.rsqrt(jnp.maximum(_sel_right(kk_t * kk_t, ones_bd), 1e-24))
        kk.append(kk_t)
        dec.append(jnp.exp(logw))
        beta.append(kk_t * a)
        kt.append(k_t * (1.0 + (a - 1.0) * ka_ref[...]))
        r.append(cs[:, 0:D_HEADS])
        v.append(cs[:, 2 * D_HEADS:3 * D_HEADS])
    sh_ref[...] = z_ref[T - 1, :, C_RW:C_RW + D_SHIFT]

    def span(x, lo, hi):
        out = None
        for i in range(lo, hi + 1):
            out = x[i] if out is None else out * x[i]
        return out

    def scaled(x, p):
        return x if p is None else x * p

    for t in range(T):
        lhs_hg[t] = qf[t] * span(fg, 0, t)
        lhs_rw[t] = scaled(kk[t], span(dec, 0, t - 1))
        lhs_rw[T + t] = r[t] * span(dec, 0, t)
    gam_hg[...] = span(fg, 0, T - 1)
    tmp[...] = jnp.zeros(tmp.shape, F32)

    def state_queries(b, carry):
        for t in range(T):
            tmp[0, t:t + 1, :] = lhs_hg[t, pl.ds(b, 1), :]
            tmp[1, t:t + 1, :] = lhs_rw[t, pl.ds(b, 1), :]
            tmp[1, T + t:T + t + 1, :] = lhs_rw[T + t, pl.ds(b, 1), :]
        qh = tmp[0]
        ar = tmp[1]
        bo = jnp.concatenate([_dot(qh[:, hsl[h]], hg_in[b, h]) for h in heads], axis=1)
        br = jnp.concatenate([_dot_nt(ar[:, hsl[h]], rw_in[b, h]) for h in heads], axis=1)
        for t in range(T):
            base_hg[t, pl.ds(b, 1), :] = bo[t:t + 1, :]
            base_rw[t, pl.ds(b, 1), :] = br[t:t + 1, :]
            base_rw[T + t, pl.ds(b, 1), :] = br[T + t:T + t + 1, :]
        return carry

    lax.fori_loop(0, nb, state_queries, 0)

    pairs = [(t, s) for t in range(T) for s in range(t + 1)]
    att = dict(zip(pairs, _head_sums([scaled(qf[t] * kf[s], span(fg, s + 1, t)) for t, s in pairs], ones_bd)))
    for t in range(T):
        o = base_hg[t]
        for s in range(t + 1):
            o = o + att[(t, s)] * iv[s]
        ssq = _sel_right(o * o, ones_bd)
        o = o * lax.rsqrt(ssq * (1.0 / HD) + NORM_EPS) * hgn_ref[...]
        o_ref[t, :, 0:D_HEADS] = (o * _silu(z_ref[t, :, C_GA:C_GA + D_HEADS])).astype(BF16)
        upd_hg[t] = scaled(kf[t], span(fg, t + 1, T - 1))
        upd_hg[T + t] = iv[t]

    strict = [(t, s) for t in range(T) for s in range(t)]
    terms = ([scaled(kk[t] * beta[s], span(dec, s + 1, t - 1)) for t, s in strict]
             + [scaled(kk[t] * kt[s], span(dec, s + 1, t - 1)) for t, s in strict]
             + [scaled(r[t] * beta[s], span(dec, s + 1, t)) for t, s in pairs]
             + [scaled(r[t] * kt[s], span(dec, s + 1, t)) for t, s in pairs]
             + [r[t] * kt[t] * rk_ref[...] for t in range(T)])
    sums = _head_sums(terms, ones_bd)
    ns, npairs = len(strict), len(pairs)
    m_b = dict(zip(strict, sums[0:ns]))
    m_k = dict(zip(strict, sums[ns:2 * ns]))
    n_b = dict(zip(pairs, sums[2 * ns:2 * ns + npairs]))
    n_k = dict(zip(pairs, sums[2 * ns + npairs:2 * ns + 2 * npairs]))
    bonus = sums[2 * ns + 2 * npairs:]
    u = []
    for t in range(T):
        u_t = base_rw[t]
        for s in range(t):
            u_t = u_t + m_k[(t, s)] * v[s] - m_b[(t, s)] * u[s]
        u.append(u_t)
    for t in range(T):
        y = base_rw[T + t]
        for s in range(t + 1):
            y = y + n_k[(t, s)] * v[s] - n_b[(t, s)] * u[s]
        yc = y - _sel_right(y, ones_bd) * (1.0 / HD)
        var = _sel_right(yc * yc, ones_bd) * (1.0 / HD)
        yn = yc * lax.rsqrt(var + RW_GN_EPS) * gng_ref[...] + gnb_ref[...] + bonus[t] * v[t]
        o_ref[t, :, D_HEADS:2 * D_HEADS] = (yn * _silu(z_ref[t, :, C_GB:C_GB + D_HEADS])).astype(BF16)
        tail = span(dec, t + 1, T - 1)
        upd_rw[0, t] = v[t]
        upd_rw[0, T + t] = u[t]
        upd_rw[1, t] = scaled(kt[t], tail)
        upd_rw[1, T + t] = -scaled(beta[t], tail)
    gam_rw[...] = span(dec, 0, T - 1)

    def state_updates(b, carry):
        for t in range(T):
            tmp[0, t:t + 1, :] = upd_hg[t, pl.ds(b, 1), :]
            tmp[2, t:t + 1, :] = upd_hg[T + t, pl.ds(b, 1), :]
        tmp[0, T:T + 1, :] = gam_hg[pl.ds(b, 1), :]
        for j in range(2 * T):
            tmp[3, j:j + 1, :] = upd_rw[0, j, pl.ds(b, 1), :]
            tmp[4, j:j + 1, :] = upd_rw[1, j, pl.ds(b, 1), :]
        kg = tmp[0]
        gcol = kg.T
        kq = jnp.where(lax.broadcasted_iota(jnp.int32, kg.shape, 0) < T, kg, 0.0)
        ii = tmp[2]
        vu = tmp[3]
        kb = tmp[4]
        g_rw = gam_rw[pl.ds(b, 1), :]
        for h in heads:
            hg_ref[b, h] = gcol[hsl[h], T:T + 1] * hg_in[b, h] + _dot_tn(kq[:, hsl[h]], ii[:, hsl[h]])
            rw_ref[b, h] = rw_in[b, h] * g_rw[:, hsl[h]] + _dot_tn(vu[:, hsl[h]], kb[:, hsl[h]])
        return carry

    lax.fori_loop(0, nb, state_updates, 0)


def _mix_sample(z, hg, rw, sh, vecs, wup, aup, ones_bd):
    T, B, _ = z.shape
    assert 2 * T == 8, "the row staging buffers hold the 2T rows of one sequence in one 8-sublane tile"
    nb = SAMPLE_BLOCK
    full = lambda a: pl.BlockSpec(a.shape, lambda i, _a=a: (0,) * _a.ndim)
    st4 = pl.BlockSpec((nb, HEADS, HD, HD), lambda i: (i, 0, 0, 0))
    sh_spec = pl.BlockSpec((nb, D_SHIFT), lambda i: (i, 0))
    rest = tuple(vecs) + (wup, aup, ones_bd)
    return pl.pallas_call(
        _mix_sample_chunk_kernel,
        out_shape=(
            jax.ShapeDtypeStruct((T, B, 2 * D_HEADS), BF16),
            jax.ShapeDtypeStruct((B, HEADS, HD, HD), F32),
            jax.ShapeDtypeStruct((B, HEADS, HD, HD), F32),
            jax.ShapeDtypeStruct((B, D_SHIFT), F32),
        ),
        grid=(B // nb,),
        in_specs=[pl.BlockSpec((T, nb, D_IN), lambda i: (0, i, 0)), st4, st4, sh_spec] + [full(a) for a in rest],
        out_specs=(pl.BlockSpec((T, nb, 2 * D_HEADS), lambda i: (0, i, 0)), st4, st4, sh_spec),
        scratch_shapes=[
            pltpu.VMEM((T, nb, D_HEADS), F32),
            pltpu.VMEM((2 * T, nb, D_HEADS), F32),
            pltpu.VMEM((T, nb, D_HEADS), F32),
            pltpu.VMEM((2 * T, nb, D_HEADS), F32),
            pltpu.VMEM((2 * T, nb, D_HEADS), F32),
            pltpu.VMEM((2, 2 * T, nb, D_HEADS), F32),
            pltpu.VMEM((nb, D_HEADS), F32),
            pltpu.VMEM((nb, D_HEADS), F32),
            pltpu.VMEM((5, 8, D_HEADS), F32),
        ],
        compiler_params=pltpu.CompilerParams(
            dimension_semantics=("arbitrary",), vmem_limit_bytes=TPU_V7X_VMEM_LIMIT),
        name="mix_sample",
    )(z, hg, rw, sh, *rest)


def _s5_sample_kernel(z_ref, sre_in, sim_in, lre_ref, lim_ref, d_ref, bblk_ref, cblk_ref, g1_ref, g2_ref,
                      o_ref, sre_ref, sim_ref, bu, sall, *, nt, nb):
    u = z_ref[:, C_U:C_U + D_S5]
    bu[...] = jnp.dot(u.astype(BF16), bblk_ref[...], preferred_element_type=F32)
    s_re, s_im = _s5_scan(bu, sall, lre_ref[...], lim_ref[...], sre_in[...], sim_in[...], nt, nb)
    sre_ref[...] = s_re
    sim_ref[...] = s_im
    yg = _s5_head(u, sall, cblk_ref, d_ref, g1_ref, g2_ref)
    o_ref[...] = (yg * _silu(z_ref[:, C_GC:C_GC + D_S5])).astype(BF16)


def _s5_sample(z2d, sre, sim, lre, lim, d, bblk, cblk, g1, g2, nt, nb):
    rows = z2d.shape[0]
    return pl.pallas_call(
        functools.partial(_s5_sample_kernel, nt=nt, nb=nb),
        out_shape=(
            jax.ShapeDtypeStruct((rows, D_S5), BF16),
            jax.ShapeDtypeStruct(sre.shape, F32),
            jax.ShapeDtypeStruct(sim.shape, F32),
        ),
        scratch_shapes=[pltpu.VMEM((rows, 2 * D_S5_STATE), F32), pltpu.VMEM((rows, 2 * D_S5_STATE), F32)],
        compiler_params=pltpu.CompilerParams(vmem_limit_bytes=TPU_V7X_VMEM_LIMIT),
        name="s5_sample",
    )(z2d, sre, sim, lre, lim, d, bblk, cblk, g1, g2)


def _prep_kernel(lbraw_ref, are_ref, aim_ref, ldt_ref, bre_ref, bim_ref, lb_ref, lre_ref, lim_ref, bbre_ref, bbim_ref):
    raw = lbraw_ref[...]
    e = jnp.exp(raw - jnp.max(raw, axis=0, keepdims=True))
    sm = e / jnp.sum(e, axis=0, keepdims=True)
    depth = raw.shape[0]
    acc = jnp.zeros_like(sm[0:1])
    for l in range(depth):
        acc = acc + sm[l:l + 1]
        lb_ref[l:l + 1, :] = acc - sm[0:1]
    a_re = are_ref[...]
    a_im = aim_ref[...]
    dt = jnp.exp(ldt_ref[...])
    mag = jnp.exp(dt * a_re)
    lam_re = mag * jnp.cos(dt * a_im)
    lam_im = mag * jnp.sin(dt * a_im)
    den = a_re * a_re + a_im * a_im
    xr = lam_re - 1.0
    f_re = (xr * a_re + lam_im * a_im) / den
    f_im = (lam_im * a_re - xr * a_im) / den
    lre_ref[...] = lam_re
    lim_ref[...] = lam_im
    b_re = bre_ref[...]
    b_im = bim_ref[...]
    bbre_ref[...] = f_re * b_re - f_im * b_im
    bbim_ref[...] = f_re * b_im + f_im * b_re


def _prep(hg_lb_raw, s5_a_re, s5_a_im, s5_log_dt, s5_b_re, s5_b_im):
    depth = hg_lb_raw.shape[0]
    n = depth * S5_GROUPS * S5_P
    col = lambda a: a.reshape(n, 1)
    ldt = jnp.broadcast_to(s5_log_dt[:, :, None], (depth, S5_GROUPS, S5_P))
    lb, lre, lim, bbre, bbim = pl.pallas_call(
        _prep_kernel,
        out_shape=(
            jax.ShapeDtypeStruct((depth, D_HEADS), F32),
            jax.ShapeDtypeStruct((n, 1), F32),
            jax.ShapeDtypeStruct((n, 1), F32),
            jax.ShapeDtypeStruct((n, S5_CH), F32),
            jax.ShapeDtypeStruct((n, S5_CH), F32),
        ),
        name="param_prep",
    )(hg_lb_raw, col(s5_a_re), col(s5_a_im), col(ldt), s5_b_re.reshape(n, S5_CH), s5_b_im.reshape(n, S5_CH))
    shape3 = (depth, S5_GROUPS, S5_P)
    return (lb, lre.reshape(depth, 1, D_S5_STATE), lim.reshape(depth, 1, D_S5_STATE),
            bbre.reshape(shape3 + (S5_CH,)), bbim.reshape(shape3 + (S5_CH,)))


def _block_diag_in(bb):
    eye = jnp.eye(S5_GROUPS, dtype=F32)
    return jnp.einsum('gpc,gh->gchp', bb, eye).reshape(D_S5, D_S5_STATE)


def _block_diag_out(c):
    eye = jnp.eye(S5_GROUPS, dtype=F32)
    return jnp.einsum('gcp,gh->gphc', c, eye).reshape(D_S5_STATE, D_S5)


def kernel(x_prompt, x_sample, state_hgrn, state_rwkv, state_rwkv_shift, state_s5_re, state_s5_im, p_prompt, p_sample, g_in, w_in, hg_lb_raw, hg_norm_g, rw_mu, rw_w0, rw_w_up, rw_a0, rw_a_up, rw_k_k, rw_k_a, rw_r_k, rw_gn_g, rw_gn_b, s5_a_re, s5_a_im, s5_log_dt, s5_b_re, s5_b_im, s5_c_re, s5_c_im, s5_d, s5_glu_w1, s5_glu_w2, w_out, ple_w_proj, ple_gate_g, ple_w_gate, g_final):
    depth = w_in.shape[0]
    B, L, _ = x_prompt.shape
    BS, T, _ = x_sample.shape
    dall, hmask, trirem, rmask, ones_np = _chunk_consts()
    consts = (jnp.asarray(dall, BF16), jnp.asarray(hmask, F32), jnp.asarray(trirem, BF16),
              jnp.asarray(rmask, F32), jnp.asarray(ones_np, BF16))
    ones_bd = consts[4]

    lb_all, lam_re, lam_im, bb_re, bb_im = _prep(hg_lb_raw, s5_a_re, s5_a_im, s5_log_dt, s5_b_re, s5_b_im)
    row = lambda a: a.reshape(1, -1)
    g_final2 = row(g_final)

    xp = x_prompt.reshape(B * L, D_MODEL)
    xs = jnp.transpose(x_sample, (1, 0, 2)).reshape(T * BS, D_MODEL)
    pp = p_prompt.reshape(depth, B * L, D_PLE)
    ps = jnp.transpose(p_sample, (0, 2, 1, 3)).reshape(depth, T * BS, D_PLE)

    outs_p = [[] for _ in range(5)]
    outs_s = [[] for _ in range(5)]
    for l in range(depth):
        w_in_l = w_in[l].astype(BF16)
        wo = w_out[l].astype(BF16)
        wg = ple_w_gate[l].astype(BF16)
        wp = ple_w_proj[l].astype(BF16)
        wup = rw_w_up[l].astype(BF16)
        aup = rw_a_up[l].astype(BF16)
        bblk = jnp.concatenate([_block_diag_in(bb_re[l]), _block_diag_in(bb_im[l])], axis=1).astype(BF16)
        cblk = jnp.concatenate([_block_diag_out(s5_c_re[l]), -_block_diag_out(s5_c_im[l])], axis=0).astype(BF16)
        g1 = s5_glu_w1[l].astype(BF16)
        g2 = s5_glu_w2[l].astype(BF16)
        hr_vecs = (row(lb_all[l]), row(hg_norm_g[l]), row(rw_mu[l]), row(rw_w0[l]), row(rw_a0[l]), row(rw_k_k[l]),
                   row(rw_k_a[l]), row(rw_r_k[l]), row(rw_gn_g[l]), row(rw_gn_b[l]))
        s5_vecs = (lam_re[l], lam_im[l], row(s5_d[l]))
        final = l == depth - 1

        z = _dense_in(xp, row(g_in[l]), w_in_l)
        o, hg, rw, sh, sre, sim = _mix_prompt(
            z.reshape(B, L, D_IN), hr_vecs + s5_vecs, (wup, aup, bblk, cblk, g1, g2), consts)
        xp = _dense_out(xp, o.reshape(B * L, D_MIX), pp[l], wo, row(ple_gate_g[l]), wg, wp, g_final2, final)
        for acc, val in zip(outs_p, (hg, rw, sh, sre.reshape(B, S5_GROUPS, S5_P), sim.reshape(B, S5_GROUPS, S5_P))):
            acc.append(val)

        zs = _dense_in(xs, row(g_in[l]), w_in_l)
        o_ab, hg, rw, sh = _mix_sample(zs.reshape(T, BS, D_IN), state_hgrn[l], state_rwkv[l], state_rwkv_shift[l],
                                       hr_vecs, wup, aup, ones_bd)
        o_c, sre, sim = _s5_sample(zs, state_s5_re[l].reshape(BS, D_S5_STATE), state_s5_im[l].reshape(BS, D_S5_STATE),
                                   lam_re[l], lam_im[l], row(s5_d[l]), bblk, cblk, g1, g2, T, BS)
        os_ = jnp.concatenate([o_ab.reshape(T * BS, 2 * D_HEADS), o_c], axis=1)
        xs = _dense_out(xs, os_, ps[l], wo, row(ple_gate_g[l]), wg, wp, g_final2, final)
        for acc, val in zip(outs_s, (hg, rw, sh, sre.reshape(BS, S5_GROUPS, S5_P), sim.reshape(BS, S5_GROUPS, S5_P))):
            acc.append(val)

    y_prompt = xp.reshape(B, L, D_MODEL)
    y_sample = jnp.transpose(xs.reshape(T, BS, D_MODEL), (1, 0, 2))
    return (y_prompt, y_sample) + tuple(jnp.stack(a) for a in outs_p) + tuple(jnp.stack(a) for a in outs_s)
```

```python
import functools
import math

import jax
import jax.numpy as jnp
import numpy as np
from jax import lax
from jax.experimental import pallas as pl
from jax.experimental.pallas import tpu as pltpu

F32 = jnp.float32
BF16 = jnp.bfloat16

D_MODEL = 1024
D_PLE = 256
HEADS = 6
HD = 64
D_HEADS = HEADS * HD
LORA = 64
D_SHIFT = 3 * D_HEADS + 2 * LORA
S5_GROUPS = 16
S5_CH = 16
S5_P = 64
D_S5 = S5_GROUPS * S5_CH
D_S5_STATE = S5_GROUPS * S5_P
D_IN = 4 * D_HEADS + D_SHIFT + D_HEADS + 2 * D_S5
D_MIX = 2 * D_HEADS + D_S5

C_Q, C_F, C_I, C_GA = 0, D_HEADS, 2 * D_HEADS, 3 * D_HEADS
C_RW = 4 * D_HEADS
C_GB = C_RW + D_SHIFT
C_U = C_GB + D_HEADS
C_GC = C_U + D_S5

LB_FLOOR = 1e-12
NORM_EPS = 1e-6
RW_GN_EPS = 64e-5
RW_DECAY_SCALE = math.exp(-0.5)

CHUNK = 64
HG_BASE = 4
ROW_TILE = 512
PITCH = CHUNK + 8
TPU_V7X_VMEM_LIMIT = 56 * 1024 * 1024


def _dot(a, b):
    return jnp.dot(a.astype(BF16), b.astype(BF16), preferred_element_type=F32)


def _dot_nt(a, b):
    return lax.dot_general(a.astype(BF16), b.astype(BF16), (((1,), (1,)), ((), ())), preferred_element_type=F32)


def _dot_tn(a, b):
    return lax.dot_general(a.astype(BF16), b.astype(BF16), (((0,), (0,)), ((), ())), preferred_element_type=F32)


def _split3(x):
    hi = x.astype(BF16)
    r1 = x - hi.astype(F32)
    mid = r1.astype(BF16)
    lo = (r1 - mid.astype(F32)).astype(BF16)
    return hi, mid, lo


def _sel_left(m, x):
    hi, mid, lo = _split3(x)
    d = lambda p: jnp.dot(m, p, preferred_element_type=F32)
    return d(hi) + d(mid) + d(lo)


def _sel_right(x, m):
    hi = x.astype(BF16)
    lo = (x - hi.astype(F32)).astype(BF16)
    d = lambda p: jnp.dot(p, m, preferred_element_type=F32)
    return d(hi) + d(lo)


def _head_sum(x, m):
    return jnp.dot(x.astype(BF16), m, preferred_element_type=F32)


def _sigmoid(x):
    return jax.nn.sigmoid(x)


def _silu(x):
    return x * jax.nn.sigmoid(x)


def _rmsnorm(x, g):
    return x * lax.rsqrt(jnp.mean(x * x, axis=-1, keepdims=True) + NORM_EPS) * g


@functools.lru_cache(maxsize=None)
def _chunk_consts():
    C = CHUNK
    t = np.arange(C)[:, None]
    i = np.arange(C)[None, :]
    tri = (i <= t).astype(np.float32)
    masks = []
    m = C // 2
    while m >= HG_BASE:
        same = (t // (2 * m)) == (i // (2 * m))
        masks.append((same & ((t % (2 * m)) >= m) & ((i % (2 * m)) < m)).astype(np.float32))
        m //= 2
    masks.append((((t // HG_BASE) == (i // HG_BASE)) & (i <= t)).astype(np.float32))
    hmask = np.stack(masks, 0)
    rmask = np.stack([(i < t).astype(np.float32), (i <= t).astype(np.float32)], 0)
    ones_bd = np.kron(np.eye(HEADS, dtype=np.float32), np.ones((HD, HD), np.float32))
    hmask = np.tile(hmask, (1, 1, 4))
    rmask = np.tile(rmask, (1, 1, 4))
    lane_unit = np.arange(4 * HD)[None, :] // HD
    gmask = np.stack([np.broadcast_to(lane_unit == g, (C, 4 * HD)) for g in range(4)], 0).astype(np.float32)
    bdmask = np.kron(np.eye(4, dtype=np.float32), np.ones((HD, HD), np.float32))
    return tri, hmask, rmask, ones_bd, gmask, bdmask


N_LEVELS = 4


def _dense_in_kernel(x_ref, g_ref, w_ref, z_ref):
    h = _rmsnorm(x_ref[...], g_ref[...])
    z_ref[...] = jnp.dot(h.astype(BF16), w_ref[...], preferred_element_type=F32)


def _dense_in(x, g, w):
    rows = x.shape[0]
    tile = min(ROW_TILE, rows)
    return pl.pallas_call(
        _dense_in_kernel,
        out_shape=jax.ShapeDtypeStruct((rows, D_IN), F32),
        grid=(rows // tile,),
        in_specs=[
            pl.BlockSpec((tile, D_MODEL), lambda i: (i, 0)),
            pl.BlockSpec((1, D_MODEL), lambda i: (0, 0)),
            pl.BlockSpec((D_MODEL, D_IN), lambda i: (0, 0)),
        ],
        out_specs=pl.BlockSpec((tile, D_IN), lambda i: (i, 0)),
        compiler_params=pltpu.CompilerParams(
            dimension_semantics=("arbitrary",), vmem_limit_bytes=TPU_V7X_VMEM_LIMIT),
        name="dense_in",
    )(x, g, w)


def _dense_out_kernel(x_ref, o_ref, p_ref, wo_ref, gg_ref, wg_ref, wp_ref, gf_ref, y_ref, *, final):
    x1 = x_ref[...] + jnp.dot(o_ref[...], wo_ref[...], preferred_element_type=F32)
    gate = _sigmoid(jnp.dot(_rmsnorm(x1, gg_ref[...]).astype(BF16), wg_ref[...], preferred_element_type=F32))
    x2 = x1 + jnp.dot(p_ref[...].astype(BF16), wp_ref[...], preferred_element_type=F32) * gate
    if final:
        x2 = _rmsnorm(x2, gf_ref[...])
    y_ref[...] = x2


def _dense_out(x, o, p, wo, gg, wg, wp, gf, final):
    rows = x.shape[0]
    tile = min(ROW_TILE, rows)
    row_spec = lambda n: pl.BlockSpec((tile, n), lambda i: (i, 0))
    full = lambda a: pl.BlockSpec(a.shape, lambda i: (0, 0))
    return pl.pallas_call(
        functools.partial(_dense_out_kernel, final=final),
        out_shape=jax.ShapeDtypeStruct((rows, D_MODEL), F32),
        grid=(rows // tile,),
        in_specs=[row_spec(D_MODEL), row_spec(D_MIX), row_spec(D_PLE), full(wo), full(gg), full(wg), full(wp), full(gf)],
        out_specs=row_spec(D_MODEL),
        compiler_params=pltpu.CompilerParams(
            dimension_semantics=("arbitrary",), vmem_limit_bytes=TPU_V7X_VMEM_LIMIT),
        name="dense_out",
    )(x, o, p, wo, gg, wg, wp, gf)


def _s5_scan(bu_ref, sall_ref, lre, lim, s_re, s_im, nt, nb):
    lr = jnp.broadcast_to(lre, (nb, D_S5_STATE))
    li = jnp.broadcast_to(lim, (nb, D_S5_STATE))

    def step(t, carry):
        sr, si = carry
        r0 = pl.multiple_of(t * nb, nb)
        br = bu_ref[pl.ds(r0, nb), 0:D_S5_STATE]
        bi = bu_ref[pl.ds(r0, nb), D_S5_STATE:2 * D_S5_STATE]
        nr = lr * sr - li * si + br
        ni = lr * si + li * sr + bi
        sall_ref[pl.ds(r0, nb), 0:D_S5_STATE] = nr
        sall_ref[pl.ds(r0, nb), D_S5_STATE:2 * D_S5_STATE] = ni
        return nr, ni

    return lax.fori_loop(0, nt, step, (s_re, s_im))


def _s5_head(u, sall_ref, cblk_ref, d_ref, g1_ref, g2_ref):
    y = jnp.dot(sall_ref[...].astype(BF16), cblk_ref[...], preferred_element_type=F32) + d_ref[...] * u
    y = jax.nn.gelu(y)
    yb = y.astype(BF16)
    return jnp.dot(yb, g1_ref[...], preferred_element_type=F32) * _sigmoid(
        jnp.dot(yb, g2_ref[...], preferred_element_type=F32))


PACK = 4
PW = PACK * HD
SEQ_PAIRS = 2


def _mix_prompt_packed_kernel(z_ref, lb_ref, hgn_ref, mu_ref, w0_ref, a0_ref, kk_ref, ka_ref, rk_ref, gng_ref,
                              gnb_ref, lre_ref, lim_ref, d_ref, wup_ref, aup_ref, bblk_ref, cblk_ref, g1_ref, g2_ref,
                              tri_ref, hmask_ref, rmask_ref, ones_ref, gmask_ref, bdmask_ref,
                              o_ref, hg_ref, rw_ref, sh_ref, sre_ref, sim_ref,
                              hg_bd0, hg_bd1, hg_bd2, rw_bd0, rw_bd1, rw_bd2, upad, up, bu, sall, ypad):
    nb = z_ref.shape[0]
    C = CHUNK
    n_pairs = D_HEADS // 128
    hg_bd = (hg_bd0, hg_bd1, hg_bd2)
    rw_bd = (rw_bd0, rw_bd1, rw_bd2)
    step = pl.program_id(0)

    @pl.when(step == 0)
    def _():
        for ref in hg_bd + rw_bd:
            ref[...] = jnp.zeros(ref.shape, F32)
        sh_ref[...] = jnp.zeros(sh_ref.shape, F32)
        sre_ref[...] = jnp.zeros(sre_ref.shape, F32)
        sim_ref[...] = jnp.zeros(sim_ref.shape, F32)

    ones_bd = ones_ref[...]
    bdmask = bdmask_ref[...]
    row_is0 = lax.broadcasted_iota(jnp.int32, (C, D_SHIFT), 0) == 0
    low_half = lax.broadcasted_iota(jnp.int32, (C, D_HEADS), 0) % 8 < HG_BASE

    def bd(x):
        m = x.shape[1] // PW
        blocks = []
        for g in range(PACK):
            mask = gmask_ref[g]
            if m > 1:
                mask = jnp.concatenate([mask] * m, axis=1)
            blocks.append(x * mask)
        return jnp.concatenate(blocks, axis=0)

    def prep(b):
        p = {}
        q = z_ref[b, :, C_Q:C_Q + D_HEADS]
        f = z_ref[b, :, C_F:C_F + D_HEADS]
        lbs = jnp.maximum(lb_ref[...], LB_FLOOR)
        logf = jnp.log(lbs + (1.0 - lbs) * _sigmoid(f))
        kf = (1.0 - lbs) * _sigmoid(-f)
        qf = _silu(q)
        cum = _sel_left(tri_ref[...], logf)
        p["qin"] = (qf * jnp.exp(cum)).astype(BF16)
        p["kend"] = (kf * jnp.exp(cum[C - 1:C, :] - cum)).astype(BF16)
        for l in range(N_LEVELS):
            m = C >> (l + 1)
            ref = jnp.concatenate([jnp.broadcast_to(cum[q0 + m - 1:q0 + m, :], (2 * m, D_HEADS))
                                   for q0 in range(0, C, 2 * m)], axis=0)
            e = jnp.exp(-jnp.abs(cum - ref))
            p["qe%d" % l] = (qf * e).astype(BF16)
            p["ke%d" % l] = (kf * e).astype(BF16)
        first = jnp.concatenate([jnp.broadcast_to(cum[q0:q0 + 1, :], (8, D_HEADS)) for q0 in range(0, C, 8)], axis=0)
        second = jnp.concatenate([jnp.broadcast_to(cum[q0 + HG_BASE:q0 + HG_BASE + 1, :], (8, D_HEADS))
                                  for q0 in range(0, C, 8)], axis=0)
        dq = cum - jnp.where(low_half, first, second)
        p["qe%d" % N_LEVELS] = (qf * jnp.exp(dq)).astype(BF16)
        p["ke%d" % N_LEVELS] = (kf * jnp.exp(-dq)).astype(BF16)
        p["gcol"] = jnp.exp(cum[C - 8:C, :].T)[:, 7:8]
        p["ib"] = z_ref[b, :, C_I:C_I + D_HEADS].astype(BF16)

        c = z_ref[b, :, C_RW:C_RW + D_SHIFT]
        prev = jnp.where(row_is0, sh_ref[pl.ds(b, 1), :], pltpu.roll(c, 1, axis=0))
        p["last"] = c[C - 1:C, :]
        cs = c + mu_ref[...] * (prev - c)
        r = cs[:, 0:D_HEADS]
        k = cs[:, D_HEADS:2 * D_HEADS]
        v = cs[:, 2 * D_HEADS:3 * D_HEADS]
        wd = cs[:, 3 * D_HEADS:3 * D_HEADS + LORA]
        ad = cs[:, 3 * D_HEADS + LORA:D_SHIFT]
        logw = -RW_DECAY_SCALE * _sigmoid(w0_ref[...] + _dot(jnp.tanh(wd), wup_ref[...]))
        a = _sigmoid(a0_ref[...] + _dot(ad, aup_ref[...]))
        kk = k * kk_ref[...]
        kk = kk * lax.rsqrt(jnp.maximum(_head_sum(kk * kk, ones_bd), 1e-24))
        kt = k * (1.0 + (a - 1.0) * ka_ref[...])
        beta = kk * a
        p["bonus"] = _head_sum(r * kt * rk_ref[...], ones_bd)
        cl = _sel_left(tri_ref[...], logw)
        p["aq"] = kk * jnp.exp(cl - logw)
        p["rq"] = r * jnp.exp(cl)
        ecl = jnp.exp(-cl)
        p["kd"] = (kt * ecl).astype(BF16)
        p["bd"] = (beta * ecl).astype(BF16)
        eend = jnp.exp(cl[C - 1:C, :] - cl)
        p["ktl"] = kt * eend
        p["btl"] = beta * eend
        p["gam"] = jnp.exp(cl[C - 1:C, :])
        p["v"] = v
        return p

    def seq_group(i, carry):
        b0 = 2 * SEQ_PAIRS * i
        ps = [prep(b0 + q) for q in range(2 * SEQ_PAIRS)]
        pairs = range(SEQ_PAIRS * n_pairs)
        pj = [u % n_pairs for u in pairs]
        pq = [2 * (u // n_pairs) for u in pairs]
        tiles = [slice(pj[u] * 128, (pj[u] + 1) * 128) for u in pairs]
        slot = [SEQ_PAIRS * i + u // n_pairs for u in pairs]

        def x4(name, rows=False):
            if rows:
                return [jnp.concatenate([ps[pq[u]][name][tiles[u], :], ps[pq[u] + 1][name][tiles[u], :]], axis=0)
                        for u in pairs]
            return [jnp.concatenate([ps[pq[u]][name][:, tiles[u]], ps[pq[u] + 1][name][:, tiles[u]]], axis=1)
                    for u in pairs]

        s_hg = [hg_bd[pj[u]][slot[u]] for u in pairs]
        s_rw = [rw_bd[pj[u]][slot[u]] for u in pairs]
        aq4, rq4, v4, btl4, ktl4 = x4("aq"), x4("rq"), x4("v"), x4("btl"), x4("ktl")
        bd4, kd4 = x4("bd"), x4("kd")
        ar = [jnp.concatenate([aq4[j], rq4[j]], axis=0).astype(BF16) for j in pairs]
        zb = [_dot_nt(ar[j], bd(bd4[j])) for j in pairs]
        zk = [_dot_nt(ar[j], bd(kd4[j])) for j in pairs]
        mb = [zb[j][0:C] * rmask_ref[0] for j in pairs]
        nbm = [zb[j][C:2 * C] * rmask_ref[1] for j in pairs]
        mk = [zk[j][0:C] * rmask_ref[0] for j in pairs]
        nk = [zk[j][C:2 * C] * rmask_ref[1] for j in pairs]
        bdv = [bd(v4[j].astype(BF16)) for j in pairs]
        x0 = [jnp.concatenate([aq4[j], _dot(mk[j], bdv[j])], axis=1) for j in pairs]
        bm = [-m for m in mb]
        pw = [_dot(mb[j], bd(mb[j].astype(BF16))) for j in pairs]
        n = 2
        while 2 * n < C:
            both = [_dot(jnp.concatenate([bm[j], pw[j]], axis=0), bd(pw[j].astype(BF16))) for j in pairs]
            bm = [bm[j] + pw[j] + both[j][0:C] for j in pairs]
            pw = [both[j][C:2 * C] for j in pairs]
            n *= 2
        bm = [bm[j] + pw[j] + _dot(bm[j], bd(pw[j].astype(BF16))) for j in pairs]
        att = [None for _ in pairs]
        for l in range(N_LEVELS + 1):
            qe4, ke4 = x4("qe%d" % l), x4("ke%d" % l)
            for j in pairs:
                term = hmask_ref[l] * _dot_nt(qe4[j], bd(ke4[j]))
                att[j] = term if att[j] is None else att[j] + term
        ib4, qin4, kend4 = x4("ib"), x4("qin"), x4("kend")
        o4 = [_dot(att[j], bd(ib4[j])) + _dot(qin4[j], s_hg[j]) for j in pairs]
        gcol4 = x4("gcol", rows=True)
        new_hg = [gcol4[j] * s_hg[j] + bdmask * _dot_tn(kend4[j], ib4[j]) for j in pairs]
        x = [x0[j] + _dot(bm[j], bd(x0[j].astype(BF16))) for j in pairs]
        bdx = [bd(x[j].astype(BF16)) for j in pairs]
        w = [bdmask * _dot_tn(btl4[j], x[j][:, 0:PW]) for j in pairs]
        gt = [bdmask * _dot_tn(jnp.concatenate([v4[j], x[j][:, PW:2 * PW]], axis=0),
                               jnp.concatenate([ktl4[j], -btl4[j]], axis=0)) for j in pairs]
        nx = [_dot(nbm[j], bdx[j]) for j in pairs]
        y4 = [_dot_nt(rq4[j] - nx[j][:, 0:PW], s_rw[j]) + _dot(nk[j], bdv[j]) - nx[j][:, PW:2 * PW] for j in pairs]
        gam4 = x4("gam")
        new_rw = [s_rw[j] * gam4[j] - _dot_nt(s_rw[j], w[j]) + gt[j] for j in pairs]
        for j in pairs:
            hg_bd[pj[j]][slot[j]] = new_hg[j]
            rw_bd[pj[j]][slot[j]] = new_rw[j]
        o_tiles = [[o4[(u // 2) * n_pairs + j][:, (u % 2) * 128:(u % 2 + 1) * 128] for j in range(n_pairs)]
                   for u in range(2 * SEQ_PAIRS)]
        y_tiles = [[y4[(u // 2) * n_pairs + j][:, (u % 2) * 128:(u % 2 + 1) * 128] for j in range(n_pairs)]
                   for u in range(2 * SEQ_PAIRS)]
        for u in range(2 * SEQ_PAIRS):
            b = b0 + u
            o = jnp.concatenate(o_tiles[u], axis=1)
            ssq = _head_sum(o * o, ones_bd)
            o = o * lax.rsqrt(ssq * (1.0 / HD) + NORM_EPS) * hgn_ref[...]
            o_ref[b, :, 0:D_HEADS] = (o * _silu(z_ref[b, :, C_GA:C_GA + D_HEADS])).astype(BF16)
            y = jnp.concatenate(y_tiles[u], axis=1)
            yc = y - _head_sum(y, ones_bd) * (1.0 / HD)
            var = _head_sum(yc * yc, ones_bd) * (1.0 / HD)
            yn = yc * lax.rsqrt(var + RW_GN_EPS) * gng_ref[...] + gnb_ref[...] + ps[u]["bonus"] * ps[u]["v"]
            o_ref[b, :, D_HEADS:2 * D_HEADS] = (yn * _silu(z_ref[b, :, C_GB:C_GB + D_HEADS])).astype(BF16)
            sh_ref[pl.ds(b, 1), :] = ps[u]["last"]
            for jj in range(D_S5 // 128):
                upad[jj, pl.ds(pl.multiple_of(b * PITCH, 8), C), :] = z_ref[b, :, C_U + jj * 128:C_U + (jj + 1) * 128]
        return carry

    lax.fori_loop(0, nb // (2 * SEQ_PAIRS), seq_group, 0)

    @pl.when(step == pl.num_programs(0) - 1)
    def _():
        for i in range(nb // 2):
            for j in range(n_pairs):
                for g in range(PACK):
                    b = 2 * i + g // 2
                    h = 2 * j + g % 2
                    blk = slice(g * HD, (g + 1) * HD)
                    hg_ref[b, h] = hg_bd[j][i, blk, :][:, blk]
                    rw_ref[b, h] = rw_bd[j][i, blk, :][:, blk]

    for t in range(C):
        for j in range(D_S5 // 128):
            up[t * nb:(t + 1) * nb, j * 128:(j + 1) * 128] = upad[j, pl.ds(t, nb, stride=PITCH), :]
    u = up[...]
    bu[...] = jnp.dot(u.astype(BF16), bblk_ref[...], preferred_element_type=F32)
    s_re, s_im = _s5_scan(bu, sall, lre_ref[...], lim_ref[...], sre_ref[...], sim_ref[...], C, nb)
    sre_ref[...] = s_re
    sim_ref[...] = s_im
    yg = _s5_head(u, sall, cblk_ref, d_ref, g1_ref, g2_ref)
    for t in range(C):
        for j in range(D_S5 // 128):
            ypad[j, pl.ds(t, nb, stride=PITCH), :] = yg[t * nb:(t + 1) * nb, j * 128:(j + 1) * 128]
    for b in range(nb):
        for j in range(D_S5 // 128):
            gate = _silu(z_ref[b, :, C_GC + j * 128:C_GC + (j + 1) * 128])
            o_ref[b, :, 2 * D_HEADS + j * 128:2 * D_HEADS + (j + 1) * 128] = (
                ypad[j, b * PITCH:b * PITCH + C, :] * gate).astype(BF16)


def _mix_prompt(z, vecs, mats, consts):
    B, L, _ = z.shape
    nt = L // CHUNK
    full = lambda a: pl.BlockSpec(a.shape, lambda i, _n=None, _a=a: (0,) * _a.ndim)
    out_shape = (
        jax.ShapeDtypeStruct((B, L, D_MIX), BF16),
        jax.ShapeDtypeStruct((B, HEADS, HD, HD), F32),
        jax.ShapeDtypeStruct((B, HEADS, HD, HD), F32),
        jax.ShapeDtypeStruct((B, D_SHIFT), F32),
        jax.ShapeDtypeStruct((B, D_S5_STATE), F32),
        jax.ShapeDtypeStruct((B, D_S5_STATE), F32),
    )
    out_specs = (
        pl.BlockSpec((B, CHUNK, D_MIX), lambda i: (0, i, 0)),
        pl.BlockSpec((B, HEADS, HD, HD), lambda i: (0, 0, 0, 0)),
        pl.BlockSpec((B, HEADS, HD, HD), lambda i: (0, 0, 0, 0)),
        pl.BlockSpec((B, D_SHIFT), lambda i: (0, 0)),
        pl.BlockSpec((B, D_S5_STATE), lambda i: (0, 0)),
        pl.BlockSpec((B, D_S5_STATE), lambda i: (0, 0)),
    )
    args = (z,) + tuple(vecs) + tuple(mats) + tuple(consts)
    in_specs = [pl.BlockSpec((B, CHUNK, D_IN), lambda i: (0, i, 0))] + [full(a) for a in args[1:]]
    rows = B * CHUNK
    assert B % 2 == 0 and D_HEADS % 128 == 0
    assert D_HEADS // 128 == 3
    scratch = [pltpu.VMEM((B // 2, PW, PW), F32) for _ in range(6)] + [
        pltpu.VMEM((D_S5 // 128, B * PITCH, 128), F32),
        pltpu.VMEM((rows, D_S5), F32),
        pltpu.VMEM((rows, 2 * D_S5_STATE), F32),
        pltpu.VMEM((rows, 2 * D_S5_STATE), F32),
        pltpu.VMEM((D_S5 // 128, B * PITCH, 128), F32),
    ]
    return pl.pallas_call(
        _mix_prompt_packed_kernel,
        out_shape=out_shape,
        grid=(nt,),
        in_specs=in_specs,
        out_specs=out_specs,
        scratch_shapes=scratch,
        compiler_params=pltpu.CompilerParams(
            dimension_semantics=("arbitrary",), vmem_limit_bytes=TPU_V7X_VMEM_LIMIT),
        name="mix_prompt",
    )(*args)


SAMPLE_BLOCK = 16


def _head_sums(terms, ones_bd):
    nb = terms[0].shape[0]
    s = _sel_right(jnp.concatenate(terms, axis=0), ones_bd)
    return [s[i * nb:(i + 1) * nb] for i in range(len(terms))]


def _mix_sample_chunk_kernel(z_ref, hg_in, rw_in, sh_in, lb_ref, hgn_ref, mu_ref, w0_ref, a0_ref, kk_ref, ka_ref,
                             rk_ref, gng_ref, gnb_ref, wup_ref, aup_ref, ones_ref,
                             o_ref, hg_ref, rw_ref, sh_ref,
                             lhs_hg, lhs_rw, base_hg, base_rw, upd_hg, upd_rw, gam_hg, gam_rw, tmp):
    T, nb, _ = z_ref.shape
    ones_bd = ones_ref[...]
    lbs = jnp.maximum(lb_ref[...], LB_FLOOR)
    heads = range(HEADS)
    hsl = [slice(h * HD, (h + 1) * HD) for h in heads]

    fg, kf, qf, iv, kk, dec, beta, kt, r, v = ([] for _ in range(10))
    for t in range(T):
        f = z_ref[t, :, C_F:C_F + D_HEADS]
        fg.append(lbs + (1.0 - lbs) * _sigmoid(f))
        kf.append((1.0 - lbs) * _sigmoid(-f))
        qf.append(_silu(z_ref[t, :, C_Q:C_Q + D_HEADS]))
        iv.append(z_ref[t, :, C_I:C_I + D_HEADS])
        c = z_ref[t, :, C_RW:C_RW + D_SHIFT]
        prev = sh_in[...] if t == 0 else z_ref[t - 1, :, C_RW:C_RW + D_SHIFT]
        cs = c + mu_ref[...] * (prev - c)
        k_t = cs[:, D_HEADS:2 * D_HEADS]
        wd = cs[:, 3 * D_HEADS:3 * D_HEADS + LORA]
        ad = cs[:, 3 * D_HEADS + LORA:D_SHIFT]
        logw = -RW_DECAY_SCALE * _sigmoid(w0_ref[...] + _dot(jnp.tanh(wd), wup_ref[...]))
        a = _sigmoid(a0_ref[...] + _dot(ad, aup_ref[...]))
        kk_t = k_t * kk_ref[...]
        kk_t = kk_t * lax.rsqrt(jnp.maximum(_sel_right(kk_t * kk_t, ones_bd), 1e-24))
        kk.append(kk_t)
        dec.append(jnp.exp(logw))
        beta.append(kk_t * a)
        kt.append(k_t * (1.0 + (a - 1.0) * ka_ref[...]))
        r.append(cs[:, 0:D_HEADS])
        v.append(cs[:, 2 * D_HEADS:3 * D_HEADS])
    sh_ref[...] = z_ref[T - 1, :, C_RW:C_RW + D_SHIFT]

    def span(x, lo, hi):
        out = None
        for i in range(lo, hi + 1):
            out = x[i] if out is None else out * x[i]
        return out

    def scaled(x, p):
        return x if p is None else x * p

    for t in range(T):
        lhs_hg[t] = qf[t] * span(fg, 0, t)
        lhs_rw[t] = scaled(kk[t], span(dec, 0, t - 1))
        lhs_rw[T + t] = r[t] * span(dec, 0, t)
    gam_hg[...] = span(fg, 0, T - 1)
    tmp[...] = jnp.zeros(tmp.shape, F32)

    def state_queries(b, carry):
        for t in range(T):
            tmp[0, t:t + 1, :] = lhs_hg[t, pl.ds(b, 1), :]
            tmp[1, t:t + 1, :] = lhs_rw[t, pl.ds(b, 1), :]
            tmp[1, T + t:T + t + 1, :] = lhs_rw[T + t, pl.ds(b, 1), :]
        qh = tmp[0]
        ar = tmp[1]
        bo = jnp.concatenate([_dot(qh[:, hsl[h]], hg_in[b, h]) for h in heads], axis=1)
        br = jnp.concatenate([_dot_nt(ar[:, hsl[h]], rw_in[b, h]) for h in heads], axis=1)
        for t in range(T):
            base_hg[t, pl.ds(b, 1), :] = bo[t:t + 1, :]
            base_rw[t, pl.ds(b, 1), :] = br[t:t + 1, :]
            base_rw[T + t, pl.ds(b, 1), :] = br[T + t:T + t + 1, :]
        return carry

    lax.fori_loop(0, nb, state_queries, 0)

    pairs = [(t, s) for t in range(T) for s in range(t + 1)]
    att = dict(zip(pairs, _head_sums([scaled(qf[t] * kf[s], span(fg, s + 1, t)) for t, s in pairs], ones_bd)))
    for t in range(T):
        o = base_hg[t]
        for s in range(t + 1):
            o = o + att[(t, s)] * iv[s]
        ssq = _sel_right(o * o, ones_bd)
        o = o * lax.rsqrt(ssq * (1.0 / HD) + NORM_EPS) * hgn_ref[...]
        o_ref[t, :, 0:D_HEADS] = (o * _silu(z_ref[t, :, C_GA:C_GA + D_HEADS])).astype(BF16)
        upd_hg[t] = scaled(kf[t], span(fg, t + 1, T - 1))
        upd_hg[T + t] = iv[t]

    strict = [(t, s) for t in range(T) for s in range(t)]
    terms = ([scaled(kk[t] * beta[s], span(dec, s + 1, t - 1)) for t, s in strict]
             + [scaled(kk[t] * kt[s], span(dec, s + 1, t - 1)) for t, s in strict]
             + [scaled(r[t] * beta[s], span(dec, s + 1, t)) for t, s in pairs]
             + [scaled(r[t] * kt[s], span(dec, s + 1, t)) for t, s in pairs]
             + [r[t] * kt[t] * rk_ref[...] for t in range(T)])
    sums = _head_sums(terms, ones_bd)
    ns, npairs = len(strict), len(pairs)
    m_b = dict(zip(strict, sums[0:ns]))
    m_k = dict(zip(strict, sums[ns:2 * ns]))
    n_b = dict(zip(pairs, sums[2 * ns:2 * ns + npairs]))
    n_k = dict(zip(pairs, sums[2 * ns + npairs:2 * ns + 2 * npairs]))
    bonus = sums[2 * ns + 2 * npairs:]
    u = []
    for t in range(T):
        u_t = base_rw[t]
        for s in range(t):
            u_t = u_t + m_k[(t, s)] * v[s] - m_b[(t, s)] * u[s]
        u.append(u_t)
    for t in range(T):
        y = base_rw[T + t]
        for s in range(t + 1):
            y = y + n_k[(t, s)] * v[s] - n_b[(t, s)] * u[s]
        yc = y - _sel_right(y, ones_bd) * (1.0 / HD)
        var = _sel_right(yc * yc, ones_bd) * (1.0 / HD)
        yn = yc * lax.rsqrt(var + RW_GN_EPS) * gng_ref[...] + gnb_ref[...] + bonus[t] * v[t]
        o_ref[t, :, D_HEADS:2 * D_HEADS] = (yn * _silu(z_ref[t, :, C_GB:C_GB + D_HEADS])).astype(BF16)
        tail = span(dec, t + 1, T - 1)
        upd_rw[0, t] = v[t]
        upd_rw[0, T + t] = u[t]
        upd_rw[1, t] = scaled(kt[t], tail)
        upd_rw[1, T + t] = -scaled(beta[t], tail)
    gam_rw[...] = span(dec, 0, T - 1)

    def state_updates(b, carry):
        for t in range(T):
            tmp[0, t:t + 1, :] = upd_hg[t, pl.ds(b, 1), :]
            tmp[2, t:t + 1, :] = upd_hg[T + t, pl.ds(b, 1), :]
        tmp[0, T:T + 1, :] = gam_hg[pl.ds(b, 1), :]
        for j in range(2 * T):
            tmp[3, j:j + 1, :] = upd_rw[0, j, pl.ds(b, 1), :]
            tmp[4, j:j + 1, :] = upd_rw[1, j, pl.ds(b, 1), :]
        kg = tmp[0]
        gcol = kg.T
        kq = jnp.where(lax.broadcasted_iota(jnp.int32, kg.shape, 0) < T, kg, 0.0)
        ii = tmp[2]
        vu = tmp[3]
        kb = tmp[4]
        g_rw = gam_rw[pl.ds(b, 1), :]
        for h in heads:
            hg_ref[b, h] = gcol[hsl[h], T:T + 1] * hg_in[b, h] + _dot_tn(kq[:, hsl[h]], ii[:, hsl[h]])
            rw_ref[b, h] = rw_in[b, h] * g_rw[:, hsl[h]] + _dot_tn(vu[:, hsl[h]], kb[:, hsl[h]])
        return carry

    lax.fori_loop(0, nb, state_updates, 0)


def _mix_sample(z, hg, rw, sh, vecs, wup, aup, ones_bd):
    T, B, _ = z.shape
    assert 2 * T == 8, "the row staging buffers hold the 2T rows of one sequence in one 8-sublane tile"
    nb = SAMPLE_BLOCK
    full = lambda a: pl.BlockSpec(a.shape, lambda i, _a=a: (0,) * _a.ndim)
    st4 = pl.BlockSpec((nb, HEADS, HD, HD), lambda i: (i, 0, 0, 0))
    sh_spec = pl.BlockSpec((nb, D_SHIFT), lambda i: (i, 0))
    rest = tuple(vecs) + (wup, aup, ones_bd)
    return pl.pallas_call(
        _mix_sample_chunk_kernel,
        out_shape=(
            jax.ShapeDtypeStruct((T, B, 2 * D_HEADS), BF16),
            jax.ShapeDtypeStruct((B, HEADS, HD, HD), F32),
            jax.ShapeDtypeStruct((B, HEADS, HD, HD), F32),
            jax.ShapeDtypeStruct((B, D_SHIFT), F32),
        ),
        grid=(B // nb,),
        in_specs=[pl.BlockSpec((T, nb, D_IN), lambda i: (0, i, 0)), st4, st4, sh_spec] + [full(a) for a in rest],
        out_specs=(pl.BlockSpec((T, nb, 2 * D_HEADS), lambda i: (0, i, 0)), st4, st4, sh_spec),
        scratch_shapes=[
            pltpu.VMEM((T, nb, D_HEADS), F32),
            pltpu.VMEM((2 * T, nb, D_HEADS), F32),
            pltpu.VMEM((T, nb, D_HEADS), F32),
            pltpu.VMEM((2 * T, nb, D_HEADS), F32),
            pltpu.VMEM((2 * T, nb, D_HEADS), F32),
            pltpu.VMEM((2, 2 * T, nb, D_HEADS), F32),
            pltpu.VMEM((nb, D_HEADS), F32),
            pltpu.VMEM((nb, D_HEADS), F32),
            pltpu.VMEM((5, 8, D_HEADS), F32),
        ],
        compiler_params=pltpu.CompilerParams(
            dimension_semantics=("arbitrary",), vmem_limit_bytes=TPU_V7X_VMEM_LIMIT),
        name="mix_sample",
    )(z, hg, rw, sh, *rest)


def _s5_sample_kernel(z_ref, sre_in, sim_in, lre_ref, lim_ref, d_ref, bblk_ref, cblk_ref, g1_ref, g2_ref,
                      o_ref, sre_ref, sim_ref, bu, sall, *, nt, nb):
    u = z_ref[:, C_U:C_U + D_S5]
    bu[...] = jnp.dot(u.astype(BF16), bblk_ref[...], preferred_element_type=F32)
    s_re, s_im = _s5_scan(bu, sall, lre_ref[...], lim_ref[...], sre_in[...], sim_in[...], nt, nb)
    sre_ref[...] = s_re
    sim_ref[...] = s_im
    yg = _s5_head(u, sall, cblk_ref, d_ref, g1_ref, g2_ref)
    o_ref[...] = (yg * _silu(z_ref[:, C_GC:C_GC + D_S5])).astype(BF16)


def _s5_sample(z2d, sre, sim, lre, lim, d, bblk, cblk, g1, g2, nt, nb):
    rows = z2d.shape[0]
    return pl.pallas_call(
        functools.partial(_s5_sample_kernel, nt=nt, nb=nb),
        out_shape=(
            jax.ShapeDtypeStruct((rows, D_S5), BF16),
            jax.ShapeDtypeStruct(sre.shape, F32),
            jax.ShapeDtypeStruct(sim.shape, F32),
        ),
        scratch_shapes=[pltpu.VMEM((rows, 2 * D_S5_STATE), F32), pltpu.VMEM((rows, 2 * D_S5_STATE), F32)],
        compiler_params=pltpu.CompilerParams(vmem_limit_bytes=TPU_V7X_VMEM_LIMIT),
        name="s5_sample",
    )(z2d, sre, sim, lre, lim, d, bblk, cblk, g1, g2)


def _prep_kernel(lbraw_ref, are_ref, aim_ref, ldt_ref, bre_ref, bim_ref, lb_ref, lre_ref, lim_ref, bbre_ref, bbim_ref):
    raw = lbraw_ref[...]
    e = jnp.exp(raw - jnp.max(raw, axis=0, keepdims=True))
    sm = e / jnp.sum(e, axis=0, keepdims=True)
    depth = raw.shape[0]
    acc = jnp.zeros_like(sm[0:1])
    for l in range(depth):
        acc = acc + sm[l:l + 1]
        lb_ref[l:l + 1, :] = acc - sm[0:1]
    a_re = are_ref[...]
    a_im = aim_ref[...]
    dt = jnp.exp(ldt_ref[...])
    mag = jnp.exp(dt * a_re)
    lam_re = mag * jnp.cos(dt * a_im)
    lam_im = mag * jnp.sin(dt * a_im)
    den = a_re * a_re + a_im * a_im
    xr = lam_re - 1.0
    f_re = (xr * a_re + lam_im * a_im) / den
    f_im = (lam_im * a_re - xr * a_im) / den
    lre_ref[...] = lam_re
    lim_ref[...] = lam_im
    b_re = bre_ref[...]
    b_im = bim_ref[...]
    bbre_ref[...] = f_re * b_re - f_im * b_im
    bbim_ref[...] = f_re * b_im + f_im * b_re


def _prep(hg_lb_raw, s5_a_re, s5_a_im, s5_log_dt, s5_b_re, s5_b_im):
    depth = hg_lb_raw.shape[0]
    n = depth * S5_GROUPS * S5_P
    col = lambda a: a.reshape(n, 1)
    ldt = jnp.broadcast_to(s5_log_dt[:, :, None], (depth, S5_GROUPS, S5_P))
    lb, lre, lim, bbre, bbim = pl.pallas_call(
        _prep_kernel,
        out_shape=(
            jax.ShapeDtypeStruct((depth, D_HEADS), F32),
            jax.ShapeDtypeStruct((n, 1), F32),
            jax.ShapeDtypeStruct((n, 1), F32),
            jax.ShapeDtypeStruct((n, S5_CH), F32),
            jax.ShapeDtypeStruct((n, S5_CH), F32),
        ),
        name="param_prep",
    )(hg_lb_raw, col(s5_a_re), col(s5_a_im), col(ldt), s5_b_re.reshape(n, S5_CH), s5_b_im.reshape(n, S5_CH))
    shape3 = (depth, S5_GROUPS, S5_P)
    return (lb, lre.reshape(depth, 1, D_S5_STATE), lim.reshape(depth, 1, D_S5_STATE),
            bbre.reshape(shape3 + (S5_CH,)), bbim.reshape(shape3 + (S5_CH,)))


def _block_diag_in(bb):
    eye = jnp.eye(S5_GROUPS, dtype=F32)
    return jnp.einsum('gpc,gh->gchp', bb, eye).reshape(D_S5, D_S5_STATE)


def _block_diag_out(c):
    eye = jnp.eye(S5_GROUPS, dtype=F32)
    return jnp.einsum('gcp,gh->gphc', c, eye).reshape(D_S5_STATE, D_S5)


def kernel(x_prompt, x_sample, state_hgrn, state_rwkv, state_rwkv_shift, state_s5_re, state_s5_im, p_prompt, p_sample, g_in, w_in, hg_lb_raw, hg_norm_g, rw_mu, rw_w0, rw_w_up, rw_a0, rw_a_up, rw_k_k, rw_k_a, rw_r_k, rw_gn_g, rw_gn_b, s5_a_re, s5_a_im, s5_log_dt, s5_b_re, s5_b_im, s5_c_re, s5_c_im, s5_d, s5_glu_w1, s5_glu_w2, w_out, ple_w_proj, ple_gate_g, ple_w_gate, g_final):
    depth = w_in.shape[0]
    B, L, _ = x_prompt.shape
    BS, T, _ = x_sample.shape
    tri, hmask, rmask, ones_np, gmask, bdmask = _chunk_consts()
    ones_bd = jnp.asarray(ones_np, BF16)
    consts = (jnp.asarray(tri, BF16), jnp.asarray(hmask, F32), jnp.asarray(rmask, F32), ones_bd,
              jnp.asarray(gmask, BF16), jnp.asarray(bdmask, F32))

    lb_all, lam_re, lam_im, bb_re, bb_im = _prep(hg_lb_raw, s5_a_re, s5_a_im, s5_log_dt, s5_b_re, s5_b_im)
    row = lambda a: a.reshape(1, -1)
    g_final2 = row(g_final)

    xp = x_prompt.reshape(B * L, D_MODEL)
    xs = jnp.transpose(x_sample, (1, 0, 2)).reshape(T * BS, D_MODEL)
    pp = p_prompt.reshape(depth, B * L, D_PLE)
    ps = jnp.transpose(p_sample, (0, 2, 1, 3)).reshape(depth, T * BS, D_PLE)

    outs_p = [[] for _ in range(5)]
    outs_s = [[] for _ in range(5)]
    for l in range(depth):
        w_in_l = w_in[l].astype(BF16)
        wo = w_out[l].astype(BF16)
        wg = ple_w_gate[l].astype(BF16)
        wp = ple_w_proj[l].astype(BF16)
        wup = rw_w_up[l].astype(BF16)
        aup = rw_a_up[l].astype(BF16)
        bblk = jnp.concatenate([_block_diag_in(bb_re[l]), _block_diag_in(bb_im[l])], axis=1).astype(BF16)
        cblk = jnp.concatenate([_block_diag_out(s5_c_re[l]), -_block_diag_out(s5_c_im[l])], axis=0).astype(BF16)
        g1 = s5_glu_w1[l].astype(BF16)
        g2 = s5_glu_w2[l].astype(BF16)
        hr_vecs = (row(lb_all[l]), row(hg_norm_g[l]), row(rw_mu[l]), row(rw_w0[l]), row(rw_a0[l]), row(rw_k_k[l]),
                   row(rw_k_a[l]), row(rw_r_k[l]), row(rw_gn_g[l]), row(rw_gn_b[l]))
        s5_vecs = (lam_re[l], lam_im[l], row(s5_d[l]))
        final = l == depth - 1

        z = _dense_in(xp, row(g_in[l]), w_in_l)
        o, hg, rw, sh, sre, sim = _mix_prompt(
            z.reshape(B, L, D_IN), hr_vecs + s5_vecs, (wup, aup, bblk, cblk, g1, g2), consts)
        xp = _dense_out(xp, o.reshape(B * L, D_MIX), pp[l], wo, row(ple_gate_g[l]), wg, wp, g_final2, final)
        for acc, val in zip(outs_p, (hg, rw, sh, sre.reshape(B, S5_GROUPS, S5_P), sim.reshape(B, S5_GROUPS, S5_P))):
            acc.append(val)

        zs = _dense_in(xs, row(g_in[l]), w_in_l)
        o_ab, hg, rw, sh = _mix_sample(zs.reshape(T, BS, D_IN), state_hgrn[l], state_rwkv[l], state_rwkv_shift[l],
                                       hr_vecs, wup, aup, ones_bd)
        o_c, sre, sim = _s5_sample(zs, state_s5_re[l].reshape(BS, D_S5_STATE), state_s5_im[l].reshape(BS, D_S5_STATE),
                                   lam_re[l], lam_im[l], row(s5_d[l]), bblk, cblk, g1, g2, T, BS)
        os_ = jnp.concatenate([o_ab.reshape(T * BS, 2 * D_HEADS), o_c], axis=1)
        xs = _dense_out(xs, os_, ps[l], wo, row(ple_gate_g[l]), wg, wp, g_final2, final)
        for acc, val in zip(outs_s, (hg, rw, sh, sre.reshape(BS, S5_GROUPS, S5_P), sim.reshape(BS, S5_GROUPS, S5_P))):
            acc.append(val)

    y_prompt = xp.reshape(B, L, D_MODEL)
    y_sample = jnp.transpose(xs.reshape(T, BS, D_MODEL), (1, 0, 2))
    return (y_prompt, y_sample) + tuple(jnp.stack(a) for a in outs_p) + tuple(jnp.stack(a) for a in outs_s)
```

```python
import functools
import math

import jax
import jax.numpy as jnp
import numpy as np
from jax import lax
from jax.experimental import pallas as pl
from jax.experimental.pallas import tpu as pltpu

F32 = jnp.float32
BF16 = jnp.bfloat16

D_MODEL = 1024
D_PLE = 256
HEADS = 6
HD = 64
D_HEADS = HEADS * HD
LORA = 64
D_SHIFT = 3 * D_HEADS + 2 * LORA
S5_GROUPS = 16
S5_CH = 16
S5_P = 64
D_S5 = S5_GROUPS * S5_CH
D_S5_STATE = S5_GROUPS * S5_P
D_IN = 4 * D_HEADS + D_SHIFT + D_HEADS + 2 * D_S5
D_MIX = 2 * D_HEADS + D_S5

C_Q, C_F, C_I, C_GA = 0, D_HEADS, 2 * D_HEADS, 3 * D_HEADS
C_RW = 4 * D_HEADS
C_GB = C_RW + D_SHIFT
C_U = C_GB + D_HEADS
C_GC = C_U + D_S5

LB_FLOOR = 1e-12
NORM_EPS = 1e-6
RW_GN_EPS = 64e-5
RW_DECAY_SCALE = math.exp(-0.5)

CHUNK = 64
HG_BASE = 4
ROW_TILE = 512
PITCH = CHUNK + 8
TPU_V7X_VMEM_LIMIT = 56 * 1024 * 1024


def _dot(a, b):
    return jnp.dot(a.astype(BF16), b.astype(BF16), preferred_element_type=F32)


def _dot_nt(a, b):
    return lax.dot_general(a.astype(BF16), b.astype(BF16), (((1,), (1,)), ((), ())), preferred_element_type=F32)


def _dot_tn(a, b):
    return lax.dot_general(a.astype(BF16), b.astype(BF16), (((0,), (0,)), ((), ())), preferred_element_type=F32)


def _split3(x):
    hi = x.astype(BF16)
    r1 = x - hi.astype(F32)
    mid = r1.astype(BF16)
    lo = (r1 - mid.astype(F32)).astype(BF16)
    return hi, mid, lo


def _sel_left(m, x):
    hi, mid, lo = _split3(x)
    d = lambda p: jnp.dot(m, p, preferred_element_type=F32)
    return d(hi) + d(mid) + d(lo)


def _sel_right(x, m):
    hi = x.astype(BF16)
    lo = (x - hi.astype(F32)).astype(BF16)
    d = lambda p: jnp.dot(p, m, preferred_element_type=F32)
    return d(hi) + d(lo)


def _head_sum(x, m):
    return jnp.dot(x.astype(BF16), m, preferred_element_type=F32)


def _sigmoid(x):
    return jax.nn.sigmoid(x)


def _silu(x):
    return x * jax.nn.sigmoid(x)


def _rmsnorm(x, g):
    return x * lax.rsqrt(jnp.mean(x * x, axis=-1, keepdims=True) + NORM_EPS) * g


@functools.lru_cache(maxsize=None)
def _chunk_consts():
    C = CHUNK
    t = np.arange(C)[:, None]
    i = np.arange(C)[None, :]
    tri = (i <= t).astype(np.float32)
    masks = []
    m = C // 2
    while m >= HG_BASE:
        same = (t // (2 * m)) == (i // (2 * m))
        masks.append((same & ((t % (2 * m)) >= m) & ((i % (2 * m)) < m)).astype(np.float32))
        m //= 2
    masks.append((((t // HG_BASE) == (i // HG_BASE)) & (i <= t)).astype(np.float32))
    hmask = np.stack(masks, 0)
    rmask = np.stack([(i < t).astype(np.float32), (i <= t).astype(np.float32)], 0)
    ones_bd = np.kron(np.eye(HEADS, dtype=np.float32), np.ones((HD, HD), np.float32))
    hmask = np.tile(hmask, (1, 1, 4))
    rmask = np.tile(rmask, (1, 1, 4))
    lane_unit = np.arange(4 * HD)[None, :] // HD
    gmask = np.stack([np.broadcast_to(lane_unit == g, (C, 4 * HD)) for g in range(4)], 0).astype(np.float32)
    bdmask = np.kron(np.eye(4, dtype=np.float32), np.ones((HD, HD), np.float32))
    return tri, hmask, rmask, ones_bd, gmask, bdmask


N_LEVELS = 4


def _dense_in_kernel(x_ref, g_ref, w_ref, z_ref):
    h = _rmsnorm(x_ref[...], g_ref[...])
    z_ref[...] = jnp.dot(h.astype(BF16), w_ref[...], preferred_element_type=F32)


def _layer(a, l):
    return pl.BlockSpec((None,) + a.shape[1:], lambda i, _n=a.ndim - 1: (l,) + (0,) * _n)


def _whole(a):
    return pl.BlockSpec(a.shape, lambda i, _n=a.ndim: (0,) * _n)


def _dense_in(x, g, w, l):
    rows = x.shape[0]
    tile = min(ROW_TILE, rows)
    return pl.pallas_call(
        _dense_in_kernel,
        out_shape=jax.ShapeDtypeStruct((rows, D_IN), F32),
        grid=(rows // tile,),
        in_specs=[pl.BlockSpec((tile, D_MODEL), lambda i: (i, 0)), _layer(g, l), _layer(w, l)],
        out_specs=pl.BlockSpec((tile, D_IN), lambda i: (i, 0)),
        compiler_params=pltpu.CompilerParams(
            dimension_semantics=("arbitrary",), vmem_limit_bytes=TPU_V7X_VMEM_LIMIT),
        name="dense_in",
    )(x, g, w)


def _dense_out_kernel(x_ref, o_ref, p_ref, wo_ref, gg_ref, wg_ref, wp_ref, gf_ref, y_ref, *, final):
    x1 = x_ref[...] + jnp.dot(o_ref[...], wo_ref[...], preferred_element_type=F32)
    gate = _sigmoid(jnp.dot(_rmsnorm(x1, gg_ref[...]).astype(BF16), wg_ref[...], preferred_element_type=F32))
    x2 = x1 + jnp.dot(p_ref[...].astype(BF16), wp_ref[...], preferred_element_type=F32) * gate
    if final:
        x2 = _rmsnorm(x2, gf_ref[...])
    y_ref[...] = x2


def _dense_out(x, o, p, wo, gg, wg, wp, gf, l, final):
    rows = x.shape[0]
    tile = min(ROW_TILE, rows)
    row_spec = lambda n: pl.BlockSpec((tile, n), lambda i: (i, 0))
    p_spec = pl.BlockSpec((None, tile, D_PLE), lambda i: (l, i, 0))
    return pl.pallas_call(
        functools.partial(_dense_out_kernel, final=final),
        out_shape=jax.ShapeDtypeStruct((rows, D_MODEL), F32),
        grid=(rows // tile,),
        in_specs=[row_spec(D_MODEL), row_spec(D_MIX), p_spec, _layer(wo, l), _layer(gg, l), _layer(wg, l),
                  _layer(wp, l), _whole(gf)],
        out_specs=row_spec(D_MODEL),
        compiler_params=pltpu.CompilerParams(
            dimension_semantics=("arbitrary",), vmem_limit_bytes=TPU_V7X_VMEM_LIMIT),
        name="dense_out",
    )(x, o, p, wo, gg, wg, wp, gf)


def _s5_scan(bu_ref, sall_ref, lre, lim, s_re, s_im, nt, nb):
    lr = jnp.broadcast_to(lre, (nb, D_S5_STATE))
    li = jnp.broadcast_to(lim, (nb, D_S5_STATE))

    def step(t, carry):
        sr, si = carry
        r0 = pl.multiple_of(t * nb, nb)
        br = bu_ref[pl.ds(r0, nb), 0:D_S5_STATE]
        bi = bu_ref[pl.ds(r0, nb), D_S5_STATE:2 * D_S5_STATE]
        nr = lr * sr - li * si + br
        ni = lr * si + li * sr + bi
        sall_ref[pl.ds(r0, nb), 0:D_S5_STATE] = nr
        sall_ref[pl.ds(r0, nb), D_S5_STATE:2 * D_S5_STATE] = ni
        return nr, ni

    return lax.fori_loop(0, nt, step, (s_re, s_im))


def _s5_head(u, sall_ref, cblk_ref, d_ref, g1_ref, g2_ref):
    y = jnp.dot(sall_ref[...].astype(BF16), cblk_ref[...], preferred_element_type=F32) + d_ref[...] * u
    y = jax.nn.gelu(y)
    yb = y.astype(BF16)
    return jnp.dot(yb, g1_ref[...], preferred_element_type=F32) * _sigmoid(
        jnp.dot(yb, g2_ref[...], preferred_element_type=F32))


PACK = 4
PW = PACK * HD
SEQ_PAIRS = 2


def _mix_prompt_packed_kernel(z_ref, lb_ref, hgn_ref, mu_ref, w0_ref, a0_ref, kk_ref, ka_ref, rk_ref, gng_ref,
                              gnb_ref, lre_ref, lim_ref, d_ref, wup_ref, aup_ref, bblk_ref, cblk_ref, g1_ref, g2_ref,
                              tri_ref, hmask_ref, rmask_ref, ones_ref, gmask_ref, bdmask_ref,
                              o_ref, hg_ref, rw_ref, sh_ref, sre_ref, sim_ref,
                              hg_bd0, hg_bd1, hg_bd2, rw_bd0, rw_bd1, rw_bd2, upad, up, bu, sall, ypad):
    nb = z_ref.shape[0]
    C = CHUNK
    n_pairs = D_HEADS // 128
    hg_bd = (hg_bd0, hg_bd1, hg_bd2)
    rw_bd = (rw_bd0, rw_bd1, rw_bd2)
    step = pl.program_id(0)

    @pl.when(step == 0)
    def _():
        for ref in hg_bd + rw_bd:
            ref[...] = jnp.zeros(ref.shape, F32)
        sh_ref[...] = jnp.zeros(sh_ref.shape, F32)
        sre_ref[...] = jnp.zeros(sre_ref.shape, F32)
        sim_ref[...] = jnp.zeros(sim_ref.shape, F32)

    ones_bd = ones_ref[...]
    bdmask = bdmask_ref[...]
    row_is0 = lax.broadcasted_iota(jnp.int32, (C, D_SHIFT), 0) == 0
    low_half = lax.broadcasted_iota(jnp.int32, (C, D_HEADS), 0) % 8 < HG_BASE

    def bd(x):
        m = x.shape[1] // PW
        blocks = []
        for g in range(PACK):
            mask = gmask_ref[g]
            if m > 1:
                mask = jnp.concatenate([mask] * m, axis=1)
            blocks.append(x * mask)
        return jnp.concatenate(blocks, axis=0)

    def prep(b):
        p = {}
        q = z_ref[b, :, C_Q:C_Q + D_HEADS]
        f = z_ref[b, :, C_F:C_F + D_HEADS]
        lbs = jnp.maximum(lb_ref[...], LB_FLOOR)
        logf = jnp.log(lbs + (1.0 - lbs) * _sigmoid(f))
        kf = (1.0 - lbs) * _sigmoid(-f)
        qf = _silu(q)
        cum = _sel_left(tri_ref[...], logf)
        p["qin"] = (qf * jnp.exp(cum)).astype(BF16)
        p["kend"] = (kf * jnp.exp(cum[C - 1:C, :] - cum)).astype(BF16)
        for l in range(N_LEVELS):
            m = C >> (l + 1)
            ref = jnp.concatenate([jnp.broadcast_to(cum[q0 + m - 1:q0 + m, :], (2 * m, D_HEADS))
                                   for q0 in range(0, C, 2 * m)], axis=0)
            e = jnp.exp(-jnp.abs(cum - ref))
            p["qe%d" % l] = (qf * e).astype(BF16)
            p["ke%d" % l] = (kf * e).astype(BF16)
        first = jnp.concatenate([jnp.broadcast_to(cum[q0:q0 + 1, :], (8, D_HEADS)) for q0 in range(0, C, 8)], axis=0)
        second = jnp.concatenate([jnp.broadcast_to(cum[q0 + HG_BASE:q0 + HG_BASE + 1, :], (8, D_HEADS))
                                  for q0 in range(0, C, 8)], axis=0)
        dq = cum - jnp.where(low_half, first, second)
        p["qe%d" % N_LEVELS] = (qf * jnp.exp(dq)).astype(BF16)
        p["ke%d" % N_LEVELS] = (kf * jnp.exp(-dq)).astype(BF16)
        p["gcol"] = jnp.exp(cum[C - 8:C, :].T)[:, 7:8]
        p["ib"] = z_ref[b, :, C_I:C_I + D_HEADS].astype(BF16)

        c = z_ref[b, :, C_RW:C_RW + D_SHIFT]
        prev = jnp.where(row_is0, sh_ref[pl.ds(b, 1), :], pltpu.roll(c, 1, axis=0))
        p["last"] = c[C - 1:C, :]
        cs = c + mu_ref[...] * (prev - c)
        r = cs[:, 0:D_HEADS]
        k = cs[:, D_HEADS:2 * D_HEADS]
        v = cs[:, 2 * D_HEADS:3 * D_HEADS]
        wd = cs[:, 3 * D_HEADS:3 * D_HEADS + LORA]
        ad = cs[:, 3 * D_HEADS + LORA:D_SHIFT]
        logw = -RW_DECAY_SCALE * _sigmoid(w0_ref[...] + _dot(jnp.tanh(wd), wup_ref[...]))
        a = _sigmoid(a0_ref[...] + _dot(ad, aup_ref[...]))
        kk = k * kk_ref[...]
        kk = kk * lax.rsqrt(jnp.maximum(_head_sum(kk * kk, ones_bd), 1e-24))
        kt = k * (1.0 + (a - 1.0) * ka_ref[...])
        beta = kk * a
        p["bonus"] = _head_sum(r * kt * rk_ref[...], ones_bd)
        cl = _sel_left(tri_ref[...], logw)
        p["aq"] = kk * jnp.exp(cl - logw)
        p["rq"] = r * jnp.exp(cl)
        ecl = jnp.exp(-cl)
        p["kd"] = (kt * ecl).astype(BF16)
        p["bd"] = (beta * ecl).astype(BF16)
        eend = jnp.exp(cl[C - 1:C, :] - cl)
        p["ktl"] = kt * eend
        p["btl"] = beta * eend
        p["gam"] = jnp.exp(cl[C - 1:C, :])
        p["v"] = v
        return p

    def seq_group(i, carry):
        b0 = 2 * SEQ_PAIRS * i
        ps = [prep(b0 + q) for q in range(2 * SEQ_PAIRS)]
        pairs = range(SEQ_PAIRS * n_pairs)
        pj = [u % n_pairs for u in pairs]
        pq = [2 * (u // n_pairs) for u in pairs]
        tiles = [slice(pj[u] * 128, (pj[u] + 1) * 128) for u in pairs]
        slot = [SEQ_PAIRS * i + u // n_pairs for u in pairs]

        def x4(name, rows=False):
            if rows:
                return [jnp.concatenate([ps[pq[u]][name][tiles[u], :], ps[pq[u] + 1][name][tiles[u], :]], axis=0)
                        for u in pairs]
            return [jnp.concatenate([ps[pq[u]][name][:, tiles[u]], ps[pq[u] + 1][name][:, tiles[u]]], axis=1)
                    for u in pairs]

        s_hg = [hg_bd[pj[u]][slot[u]] for u in pairs]
        s_rw = [rw_bd[pj[u]][slot[u]] for u in pairs]
        aq4, rq4, v4, btl4, ktl4 = x4("aq"), x4("rq"), x4("v"), x4("btl"), x4("ktl")
        bd4, kd4 = x4("bd"), x4("kd")
        ar = [jnp.concatenate([aq4[j], rq4[j]], axis=0).astype(BF16) for j in pairs]
        zb = [_dot_nt(ar[j], bd(bd4[j])) for j in pairs]
        zk = [_dot_nt(ar[j], bd(kd4[j])) for j in pairs]
        mb = [zb[j][0:C] * rmask_ref[0] for j in pairs]
        nbm = [zb[j][C:2 * C] * rmask_ref[1] for j in pairs]
        mk = [zk[j][0:C] * rmask_ref[0] for j in pairs]
        nk = [zk[j][C:2 * C] * rmask_ref[1] for j in pairs]
        bdv = [bd(v4[j].astype(BF16)) for j in pairs]
        x0 = [jnp.concatenate([aq4[j], _dot(mk[j], bdv[j])], axis=1) for j in pairs]
        bm = [-m for m in mb]
        pw = [_dot(mb[j], bd(mb[j].astype(BF16))) for j in pairs]
        n = 2
        while 2 * n < C:
            both = [_dot(jnp.concatenate([bm[j], pw[j]], axis=0), bd(pw[j].astype(BF16))) for j in pairs]
            bm = [bm[j] + pw[j] + both[j][0:C] for j in pairs]
            pw = [both[j][C:2 * C] for j in pairs]
            n *= 2
        bm = [bm[j] + pw[j] + _dot(bm[j], bd(pw[j].astype(BF16))) for j in pairs]
        att = [None for _ in pairs]
        for l in range(N_LEVELS + 1):
            qe4, ke4 = x4("qe%d" % l), x4("ke%d" % l)
            for j in pairs:
                term = hmask_ref[l] * _dot_nt(qe4[j], bd(ke4[j]))
                att[j] = term if att[j] is None else att[j] + term
        ib4, qin4, kend4 = x4("ib"), x4("qin"), x4("kend")
        o4 = [_dot(att[j], bd(ib4[j])) + _dot(qin4[j], s_hg[j]) for j in pairs]
        gcol4 = x4("gcol", rows=True)
        new_hg = [gcol4[j] * s_hg[j] + bdmask * _dot_tn(kend4[j], ib4[j]) for j in pairs]
        x = [x0[j] + _dot(bm[j], bd(x0[j].astype(BF16))) for j in pairs]
        bdx = [bd(x[j].astype(BF16)) for j in pairs]
        w = [bdmask * _dot_tn(btl4[j], x[j][:, 0:PW]) for j in pairs]
        gt = [bdmask * _dot_tn(jnp.concatenate([v4[j], x[j][:, PW:2 * PW]], axis=0),
                               jnp.concatenate([ktl4[j], -btl4[j]], axis=0)) for j in pairs]
        nx = [_dot(nbm[j], bdx[j]) for j in pairs]
        y4 = [_dot_nt(rq4[j] - nx[j][:, 0:PW], s_rw[j]) + _dot(nk[j], bdv[j]) - nx[j][:, PW:2 * PW] for j in pairs]
        gam4 = x4("gam")
        new_rw = [s_rw[j] * gam4[j] - _dot_nt(s_rw[j], w[j]) + gt[j] for j in pairs]
        for j in pairs:
            hg_bd[pj[j]][slot[j]] = new_hg[j]
            rw_bd[pj[j]][slot[j]] = new_rw[j]
        o_tiles = [[o4[(u // 2) * n_pairs + j][:, (u % 2) * 128:(u % 2 + 1) * 128] for j in range(n_pairs)]
                   for u in range(2 * SEQ_PAIRS)]
        y_tiles = [[y4[(u // 2) * n_pairs + j][:, (u % 2) * 128:(u % 2 + 1) * 128] for j in range(n_pairs)]
                   for u in range(2 * SEQ_PAIRS)]
        for u in range(2 * SEQ_PAIRS):
            b = b0 + u
            o = jnp.concatenate(o_tiles[u], axis=1)
            ssq = _head_sum(o * o, ones_bd)
            o = o * lax.rsqrt(ssq * (1.0 / HD) + NORM_EPS) * hgn_ref[...]
            o_ref[b, :, 0:D_HEADS] = (o * _silu(z_ref[b, :, C_GA:C_GA + D_HEADS])).astype(BF16)
            y = jnp.concatenate(y_tiles[u], axis=1)
            yc = y - _head_sum(y, ones_bd) * (1.0 / HD)
            var = _head_sum(yc * yc, ones_bd) * (1.0 / HD)
            yn = yc * lax.rsqrt(var + RW_GN_EPS) * gng_ref[...] + gnb_ref[...] + ps[u]["bonus"] * ps[u]["v"]
            o_ref[b, :, D_HEADS:2 * D_HEADS] = (yn * _silu(z_ref[b, :, C_GB:C_GB + D_HEADS])).astype(BF16)
            sh_ref[pl.ds(b, 1), :] = ps[u]["last"]
            for jj in range(D_S5 // 128):
                upad[jj, pl.ds(pl.multiple_of(b * PITCH, 8), C), :] = z_ref[b, :, C_U + jj * 128:C_U + (jj + 1) * 128]
        return carry

    lax.fori_loop(0, nb // (2 * SEQ_PAIRS), seq_group, 0)

    @pl.when(step == pl.num_programs(0) - 1)
    def _():
        for i in range(nb // 2):
            for j in range(n_pairs):
                for g in range(PACK):
                    b = 2 * i + g // 2
                    h = 2 * j + g % 2
                    blk = slice(g * HD, (g + 1) * HD)
                    hg_ref[b, h] = hg_bd[j][i, blk, :][:, blk]
                    rw_ref[b, h] = rw_bd[j][i, blk, :][:, blk]

    for t in range(C):
        for j in range(D_S5 // 128):
            up[t * nb:(t + 1) * nb, j * 128:(j + 1) * 128] = upad[j, pl.ds(t, nb, stride=PITCH), :]
    u = up[...]
    bu[...] = jnp.dot(u.astype(BF16), bblk_ref[...], preferred_element_type=F32)
    s_re, s_im = _s5_scan(bu, sall, lre_ref[...], lim_ref[...], sre_ref[...], sim_ref[...], C, nb)
    sre_ref[...] = s_re
    sim_ref[...] = s_im
    yg = _s5_head(u, sall, cblk_ref, d_ref, g1_ref, g2_ref)
    for t in range(C):
        for j in range(D_S5 // 128):
            ypad[j, pl.ds(t, nb, stride=PITCH), :] = yg[t * nb:(t + 1) * nb, j * 128:(j + 1) * 128]
    for b in range(nb):
        for j in range(D_S5 // 128):
            gate = _silu(z_ref[b, :, C_GC + j * 128:C_GC + (j + 1) * 128])
            o_ref[b, :, 2 * D_HEADS + j * 128:2 * D_HEADS + (j + 1) * 128] = (
                ypad[j, b * PITCH:b * PITCH + C, :] * gate).astype(BF16)


def _mix_prompt(z, vecs, mats, consts, l):
    B, L, _ = z.shape
    nt = L // CHUNK
    out_shape = (
        jax.ShapeDtypeStruct((B, L, D_MIX), BF16),
        jax.ShapeDtypeStruct((B, HEADS, HD, HD), F32),
        jax.ShapeDtypeStruct((B, HEADS, HD, HD), F32),
        jax.ShapeDtypeStruct((B, D_SHIFT), F32),
        jax.ShapeDtypeStruct((B, D_S5_STATE), F32),
        jax.ShapeDtypeStruct((B, D_S5_STATE), F32),
    )
    out_specs = (
        pl.BlockSpec((B, CHUNK, D_MIX), lambda i: (0, i, 0)),
        pl.BlockSpec((B, HEADS, HD, HD), lambda i: (0, 0, 0, 0)),
        pl.BlockSpec((B, HEADS, HD, HD), lambda i: (0, 0, 0, 0)),
        pl.BlockSpec((B, D_SHIFT), lambda i: (0, 0)),
        pl.BlockSpec((B, D_S5_STATE), lambda i: (0, 0)),
        pl.BlockSpec((B, D_S5_STATE), lambda i: (0, 0)),
    )
    args = (z,) + tuple(vecs) + tuple(mats) + tuple(consts)
    in_specs = ([pl.BlockSpec((B, CHUNK, D_IN), lambda i: (0, i, 0))]
                + [_layer(a, l) for a in tuple(vecs) + tuple(mats)] + [_whole(a) for a in consts])
    rows = B * CHUNK
    assert B % (2 * SEQ_PAIRS) == 0 and D_HEADS // 128 == 3
    scratch = [pltpu.VMEM((B // 2, PW, PW), F32) for _ in range(6)] + [
        pltpu.VMEM((D_S5 // 128, B * PITCH, 128), F32),
        pltpu.VMEM((rows, D_S5), F32),
        pltpu.VMEM((rows, 2 * D_S5_STATE), F32),
        pltpu.VMEM((rows, 2 * D_S5_STATE), F32),
        pltpu.VMEM((D_S5 // 128, B * PITCH, 128), F32),
    ]
    return pl.pallas_call(
        _mix_prompt_packed_kernel,
        out_shape=out_shape,
        grid=(nt,),
        in_specs=in_specs,
        out_specs=out_specs,
        scratch_shapes=scratch,
        compiler_params=pltpu.CompilerParams(
            dimension_semantics=("arbitrary",), vmem_limit_bytes=TPU_V7X_VMEM_LIMIT),
        name="mix_prompt",
    )(*args)


SAMPLE_BLOCK = 16
SAMPLE_UNROLL = 4


def _head_sums(terms, ones_bd):
    nb = terms[0].shape[0]
    s = _sel_right(jnp.concatenate(terms, axis=0), ones_bd)
    return [s[i * nb:(i + 1) * nb] for i in range(len(terms))]


def _mix_sample_chunk_kernel(z_ref, hg_in, rw_in, sh_in, lb_ref, hgn_ref, mu_ref, w0_ref, a0_ref, kk_ref, ka_ref,
                             rk_ref, gng_ref, gnb_ref, wup_ref, aup_ref, ones_ref,
                             o_ref, hg_ref, rw_ref, sh_ref,
                             lhs_hg, lhs_rw, base_hg, base_rw, upd_hg, upd_rw, gam_hg, gam_rw, tmp):
    T, nb, _ = z_ref.shape
    ones_bd = ones_ref[...]
    lbs = jnp.maximum(lb_ref[...], LB_FLOOR)
    heads = range(HEADS)
    hsl = [slice(h * HD, (h + 1) * HD) for h in heads]

    fg, kf, qf, iv, kk, dec, beta, kt, r, v = ([] for _ in range(10))
    for t in range(T):
        f = z_ref[t, :, C_F:C_F + D_HEADS]
        fg.append(lbs + (1.0 - lbs) * _sigmoid(f))
        kf.append((1.0 - lbs) * _sigmoid(-f))
        qf.append(_silu(z_ref[t, :, C_Q:C_Q + D_HEADS]))
        iv.append(z_ref[t, :, C_I:C_I + D_HEADS])
        c = z_ref[t, :, C_RW:C_RW + D_SHIFT]
        prev = sh_in[...] if t == 0 else z_ref[t - 1, :, C_RW:C_RW + D_SHIFT]
        cs = c + mu_ref[...] * (prev - c)
        k_t = cs[:, D_HEADS:2 * D_HEADS]
        wd = cs[:, 3 * D_HEADS:3 * D_HEADS + LORA]
        ad = cs[:, 3 * D_HEADS + LORA:D_SHIFT]
        logw = -RW_DECAY_SCALE * _sigmoid(w0_ref[...] + _dot(jnp.tanh(wd), wup_ref[...]))
        a = _sigmoid(a0_ref[...] + _dot(ad, aup_ref[...]))
        kk_t = k_t * kk_ref[...]
        kk_t = kk_t * lax.rsqrt(jnp.maximum(_sel_right(kk_t * kk_t, ones_bd), 1e-24))
        kk.append(kk_t)
        dec.append(jnp.exp(logw))
        beta.append(kk_t * a)
        kt.append(k_t * (1.0 + (a - 1.0) * ka_ref[...]))
        r.append(cs[:, 0:D_HEADS])
        v.append(cs[:, 2 * D_HEADS:3 * D_HEADS])
    sh_ref[...] = z_ref[T - 1, :, C_RW:C_RW + D_SHIFT]

    def span(x, lo, hi):
        out = None
        for i in range(lo, hi + 1):
            out = x[i] if out is None else out * x[i]
        return out

    def scaled(x, p):
        return x if p is None else x * p

    for t in range(T):
        lhs_hg[t] = qf[t] * span(fg, 0, t)
        lhs_rw[t] = scaled(kk[t], span(dec, 0, t - 1))
        lhs_rw[T + t] = r[t] * span(dec, 0, t)
    gam_hg[...] = span(fg, 0, T - 1)
    tmp[...] = jnp.zeros(tmp.shape, F32)

    unroll = range(SAMPLE_UNROLL)

    def state_queries(i, carry):
        bs = [i * SAMPLE_UNROLL + u for u in unroll]
        for u in unroll:
            for t in range(T):
                tmp[5 * u, t:t + 1, :] = lhs_hg[t, pl.ds(bs[u], 1), :]
                tmp[5 * u + 1, t:t + 1, :] = lhs_rw[t, pl.ds(bs[u], 1), :]
                tmp[5 * u + 1, T + t:T + t + 1, :] = lhs_rw[T + t, pl.ds(bs[u], 1), :]
        qh = [tmp[5 * u] for u in unroll]
        ar = [tmp[5 * u + 1] for u in unroll]
        bo = [jnp.concatenate([_dot(qh[u][:, hsl[h]], hg_in[bs[u], h]) for h in heads], axis=1) for u in unroll]
        br = [jnp.concatenate([_dot_nt(ar[u][:, hsl[h]], rw_in[bs[u], h]) for h in heads], axis=1) for u in unroll]
        for u in unroll:
            for t in range(T):
                base_hg[t, pl.ds(bs[u], 1), :] = bo[u][t:t + 1, :]
                base_rw[t, pl.ds(bs[u], 1), :] = br[u][t:t + 1, :]
                base_rw[T + t, pl.ds(bs[u], 1), :] = br[u][T + t:T + t + 1, :]
        return carry

    lax.fori_loop(0, nb // SAMPLE_UNROLL, state_queries, 0)

    pairs = [(t, s) for t in range(T) for s in range(t + 1)]
    att = dict(zip(pairs, _head_sums([scaled(qf[t] * kf[s], span(fg, s + 1, t)) for t, s in pairs], ones_bd)))
    for t in range(T):
        o = base_hg[t]
        for s in range(t + 1):
            o = o + att[(t, s)] * iv[s]
        ssq = _sel_right(o * o, ones_bd)
        o = o * lax.rsqrt(ssq * (1.0 / HD) + NORM_EPS) * hgn_ref[...]
        o_ref[t, :, 0:D_HEADS] = (o * _silu(z_ref[t, :, C_GA:C_GA + D_HEADS])).astype(BF16)
        upd_hg[t] = scaled(kf[t], span(fg, t + 1, T - 1))
        upd_hg[T + t] = iv[t]

    strict = [(t, s) for t in range(T) for s in range(t)]
    terms = ([scaled(kk[t] * beta[s], span(dec, s + 1, t - 1)) for t, s in strict]
             + [scaled(kk[t] * kt[s], span(dec, s + 1, t - 1)) for t, s in strict]
             + [scaled(r[t] * beta[s], span(dec, s + 1, t)) for t, s in pairs]
             + [scaled(r[t] * kt[s], span(dec, s + 1, t)) for t, s in pairs]
             + [r[t] * kt[t] * rk_ref[...] for t in range(T)])
    sums = _head_sums(terms, ones_bd)
    ns, npairs = len(strict), len(pairs)
    m_b = dict(zip(strict, sums[0:ns]))
    m_k = dict(zip(strict, sums[ns:2 * ns]))
    n_b = dict(zip(pairs, sums[2 * ns:2 * ns + npairs]))
    n_k = dict(zip(pairs, sums[2 * ns + npairs:2 * ns + 2 * npairs]))
    bonus = sums[2 * ns + 2 * npairs:]
    u = []
    for t in range(T):
        u_t = base_rw[t]
        for s in range(t):
            u_t = u_t + m_k[(t, s)] * v[s] - m_b[(t, s)] * u[s]
        u.append(u_t)
    for t in range(T):
        y = base_rw[T + t]
        for s in range(t + 1):
            y = y + n_k[(t, s)] * v[s] - n_b[(t, s)] * u[s]
        yc = y - _sel_right(y, ones_bd) * (1.0 / HD)
        var = _sel_right(yc * yc, ones_bd) * (1.0 / HD)
        yn = yc * lax.rsqrt(var + RW_GN_EPS) * gng_ref[...] + gnb_ref[...] + bonus[t] * v[t]
        o_ref[t, :, D_HEADS:2 * D_HEADS] = (yn * _silu(z_ref[t, :, C_GB:C_GB + D_HEADS])).astype(BF16)
        tail = span(dec, t + 1, T - 1)
        upd_rw[0, t] = v[t]
        upd_rw[0, T + t] = u[t]
        upd_rw[1, t] = scaled(kt[t], tail)
        upd_rw[1, T + t] = -scaled(beta[t], tail)
    gam_rw[...] = span(dec, 0, T - 1)

    is_key_row = lax.broadcasted_iota(jnp.int32, (8, D_HEADS), 0) < T

    def state_updates(i, carry):
        bs = [i * SAMPLE_UNROLL + u for u in unroll]
        for u in unroll:
            for t in range(T):
                tmp[5 * u, t:t + 1, :] = upd_hg[t, pl.ds(bs[u], 1), :]
                tmp[5 * u + 2, t:t + 1, :] = upd_hg[T + t, pl.ds(bs[u], 1), :]
            tmp[5 * u, T:T + 1, :] = gam_hg[pl.ds(bs[u], 1), :]
            for j in range(2 * T):
                tmp[5 * u + 3, j:j + 1, :] = upd_rw[0, j, pl.ds(bs[u], 1), :]
                tmp[5 * u + 4, j:j + 1, :] = upd_rw[1, j, pl.ds(bs[u], 1), :]
        kg = [tmp[5 * u] for u in unroll]
        gcol = [k_u.T for k_u in kg]
        kq = [jnp.where(is_key_row, k_u, 0.0) for k_u in kg]
        ii = [tmp[5 * u + 2] for u in unroll]
        vu = [tmp[5 * u + 3] for u in unroll]
        kb = [tmp[5 * u + 4] for u in unroll]
        g_rw = [gam_rw[pl.ds(bs[u], 1), :] for u in unroll]
        new_hg = [[gcol[u][hsl[h], T:T + 1] * hg_in[bs[u], h] + _dot_tn(kq[u][:, hsl[h]], ii[u][:, hsl[h]])
                   for h in heads] for u in unroll]
        new_rw = [[rw_in[bs[u], h] * g_rw[u][:, hsl[h]] + _dot_tn(vu[u][:, hsl[h]], kb[u][:, hsl[h]])
                   for h in heads] for u in unroll]
        for u in unroll:
            for h in heads:
                hg_ref[bs[u], h] = new_hg[u][h]
                rw_ref[bs[u], h] = new_rw[u][h]
        return carry

    lax.fori_loop(0, nb // SAMPLE_UNROLL, state_updates, 0)


def _mix_sample(z, hg, rw, sh, vecs, wup, aup, ones_bd, l):
    T, B, _ = z.shape
    assert 2 * T == 8, "the row staging buffers hold the 2T rows of one sequence in one 8-sublane tile"
    nb = SAMPLE_BLOCK
    st4 = pl.BlockSpec((None, nb, HEADS, HD, HD), lambda i: (l, i, 0, 0, 0))
    sh_spec = pl.BlockSpec((None, nb, D_SHIFT), lambda i: (l, i, 0))
    stacked = tuple(vecs) + (wup, aup)
    rest = stacked + (ones_bd,)
    return pl.pallas_call(
        _mix_sample_chunk_kernel,
        out_shape=(
            jax.ShapeDtypeStruct((T, B, 2 * D_HEADS), BF16),
            jax.ShapeDtypeStruct(hg.shape, F32),
            jax.ShapeDtypeStruct(rw.shape, F32),
            jax.ShapeDtypeStruct(sh.shape, F32),
        ),
        grid=(B // nb,),
        in_specs=([pl.BlockSpec((T, nb, D_IN), lambda i: (0, i, 0)), st4, st4, sh_spec]
                  + [_layer(a, l) for a in stacked] + [_whole(ones_bd)]),
        out_specs=(pl.BlockSpec((T, nb, 2 * D_HEADS), lambda i: (0, i, 0)), st4, st4, sh_spec),
        input_output_aliases={1: 1, 2: 2, 3: 3},
        scratch_shapes=[
            pltpu.VMEM((T, nb, D_HEADS), F32),
            pltpu.VMEM((2 * T, nb, D_HEADS), F32),
            pltpu.VMEM((T, nb, D_HEADS), F32),
            pltpu.VMEM((2 * T, nb, D_HEADS), F32),
            pltpu.VMEM((2 * T, nb, D_HEADS), F32),
            pltpu.VMEM((2, 2 * T, nb, D_HEADS), F32),
            pltpu.VMEM((nb, D_HEADS), F32),
            pltpu.VMEM((nb, D_HEADS), F32),
            pltpu.VMEM((5 * SAMPLE_UNROLL, 8, D_HEADS), F32),
        ],
        compiler_params=pltpu.CompilerParams(
            dimension_semantics=("arbitrary",), vmem_limit_bytes=TPU_V7X_VMEM_LIMIT),
        name="mix_sample",
    )(z, hg, rw, sh, *rest)


def _s5_sample_kernel(z_ref, sre_in, sim_in, lre_ref, lim_ref, d_ref, bblk_ref, cblk_ref, g1_ref, g2_ref,
                      o_ref, sre_ref, sim_ref, bu, sall, *, nt, nb):
    u = z_ref[:, C_U:C_U + D_S5]
    bu[...] = jnp.dot(u.astype(BF16), bblk_ref[...], preferred_element_type=F32)
    s_re, s_im = _s5_scan(bu, sall, lre_ref[...], lim_ref[...], sre_in[...], sim_in[...], nt, nb)
    sre_ref[...] = s_re
    sim_ref[...] = s_im
    yg = _s5_head(u, sall, cblk_ref, d_ref, g1_ref, g2_ref)
    o_ref[...] = (yg * _silu(z_ref[:, C_GC:C_GC + D_S5])).astype(BF16)


def _s5_sample(z2d, sre, sim, stacked, nt, nb, l):
    rows = z2d.shape[0]
    return pl.pallas_call(
        functools.partial(_s5_sample_kernel, nt=nt, nb=nb),
        out_shape=(
            jax.ShapeDtypeStruct((rows, D_S5), BF16),
            jax.ShapeDtypeStruct(sre.shape, F32),
            jax.ShapeDtypeStruct(sim.shape, F32),
        ),
        grid=(1,),
        in_specs=[_whole(z2d), _layer(sre, l), _layer(sim, l)] + [_layer(a, l) for a in stacked],
        out_specs=(pl.BlockSpec((rows, D_S5), lambda i: (0, 0)), _layer(sre, l), _layer(sim, l)),
        input_output_aliases={1: 1, 2: 2},
        scratch_shapes=[pltpu.VMEM((rows, 2 * D_S5_STATE), F32), pltpu.VMEM((rows, 2 * D_S5_STATE), F32)],
        compiler_params=pltpu.CompilerParams(
            dimension_semantics=("arbitrary",), vmem_limit_bytes=TPU_V7X_VMEM_LIMIT),
        name="s5_sample",
    )(z2d, sre, sim, *stacked)


def _prep_kernel(lbraw_ref, are_ref, aim_ref, ldt_ref, bre_ref, bim_ref, lb_ref, lre_ref, lim_ref, bbre_ref, bbim_ref):
    raw = lbraw_ref[...]
    e = jnp.exp(raw - jnp.max(raw, axis=0, keepdims=True))
    sm = e / jnp.sum(e, axis=0, keepdims=True)
    depth = raw.shape[0]
    acc = jnp.zeros_like(sm[0:1])
    for l in range(depth):
        acc = acc + sm[l:l + 1]
        lb_ref[l:l + 1, :] = acc - sm[0:1]
    a_re = are_ref[...]
    a_im = aim_ref[...]
    dt = jnp.exp(ldt_ref[...])
    mag = jnp.exp(dt * a_re)
    lam_re = mag * jnp.cos(dt * a_im)
    lam_im = mag * jnp.sin(dt * a_im)
    den = a_re * a_re + a_im * a_im
    xr = lam_re - 1.0
    f_re = (xr * a_re + lam_im * a_im) / den
    f_im = (lam_im * a_re - xr * a_im) / den
    lre_ref[...] = lam_re
    lim_ref[...] = lam_im
    b_re = bre_ref[...]
    b_im = bim_ref[...]
    bbre_ref[...] = f_re * b_re - f_im * b_im
    bbim_ref[...] = f_re * b_im + f_im * b_re


def _prep(hg_lb_raw, s5_a_re, s5_a_im, s5_log_dt, s5_b_re, s5_b_im):
    depth = hg_lb_raw.shape[0]
    n = depth * S5_GROUPS * S5_P
    col = lambda a: a.reshape(n, 1)
    ldt = jnp.broadcast_to(s5_log_dt[:, :, None], (depth, S5_GROUPS, S5_P))
    lb, lre, lim, bbre, bbim = pl.pallas_call(
        _prep_kernel,
        out_shape=(
            jax.ShapeDtypeStruct((depth, D_HEADS), F32),
            jax.ShapeDtypeStruct((n, 1), F32),
            jax.ShapeDtypeStruct((n, 1), F32),
            jax.ShapeDtypeStruct((n, S5_CH), F32),
            jax.ShapeDtypeStruct((n, S5_CH), F32),
        ),
        name="param_prep",
    )(hg_lb_raw, col(s5_a_re), col(s5_a_im), col(ldt), s5_b_re.reshape(n, S5_CH), s5_b_im.reshape(n, S5_CH))
    shape3 = (depth, S5_GROUPS, S5_P)
    return (lb, lre.reshape(depth, 1, D_S5_STATE), lim.reshape(depth, 1, D_S5_STATE),
            bbre.reshape(shape3 + (S5_CH,)), bbim.reshape(shape3 + (S5_CH,)))


def kernel(x_prompt, x_sample, state_hgrn, state_rwkv, state_rwkv_shift, state_s5_re, state_s5_im, p_prompt, p_sample, g_in, w_in, hg_lb_raw, hg_norm_g, rw_mu, rw_w0, rw_w_up, rw_a0, rw_a_up, rw_k_k, rw_k_a, rw_r_k, rw_gn_g, rw_gn_b, s5_a_re, s5_a_im, s5_log_dt, s5_b_re, s5_b_im, s5_c_re, s5_c_im, s5_d, s5_glu_w1, s5_glu_w2, w_out, ple_w_proj, ple_gate_g, ple_w_gate, g_final):
    depth = w_in.shape[0]
    B, L, _ = x_prompt.shape
    BS, T, _ = x_sample.shape
    tri, hmask, rmask, ones_np, gmask, bdmask = _chunk_consts()
    ones_bd = jnp.asarray(ones_np, BF16)
    consts = (jnp.asarray(tri, BF16), jnp.asarray(hmask, F32), jnp.asarray(rmask, F32), ones_bd,
              jnp.asarray(gmask, BF16), jnp.asarray(bdmask, F32))

    lb_all, lam_re, lam_im, bb_re, bb_im = _prep(hg_lb_raw, s5_a_re, s5_a_im, s5_log_dt, s5_b_re, s5_b_im)
    g_final2 = g_final.reshape(1, -1)

    xp = x_prompt.reshape(B * L, D_MODEL)
    xs = jnp.transpose(x_sample, (1, 0, 2)).reshape(T * BS, D_MODEL)
    pp = p_prompt.reshape(depth, B * L, D_PLE)
    ps = jnp.transpose(p_sample, (0, 2, 1, 3)).reshape(depth, T * BS, D_PLE)

    rows3 = lambda a: a.reshape(depth, 1, -1)
    cast = lambda a: a.astype(BF16)
    g_in3, gate_g3 = rows3(g_in), rows3(ple_gate_g)
    w_in_b, wo, wg, wp = cast(w_in), cast(w_out), cast(ple_w_gate), cast(ple_w_proj)
    wup, aup, g1, g2 = cast(rw_w_up), cast(rw_a_up), cast(s5_glu_w1), cast(s5_glu_w2)
    eye = jnp.eye(S5_GROUPS, dtype=F32)
    bd_in = lambda bb: jnp.einsum('lgpc,gh->lgchp', bb, eye).reshape(depth, D_S5, D_S5_STATE)
    bd_out = lambda c: jnp.einsum('lgcp,gh->lgphc', c, eye).reshape(depth, D_S5_STATE, D_S5)
    bblk = cast(jnp.concatenate([bd_in(bb_re), bd_in(bb_im)], axis=2))
    cblk = cast(jnp.concatenate([bd_out(s5_c_re), -bd_out(s5_c_im)], axis=1))
    hr_vecs = tuple(rows3(a) for a in (lb_all, hg_norm_g, rw_mu, rw_w0, rw_a0, rw_k_k, rw_k_a, rw_r_k,
                                       rw_gn_g, rw_gn_b))
    s5_vecs = (lam_re, lam_im, rows3(s5_d))

    hg_s, rw_s, sh_s = state_hgrn, state_rwkv, state_rwkv_shift
    sre_s = state_s5_re.reshape(depth, BS, D_S5_STATE)
    sim_s = state_s5_im.reshape(depth, BS, D_S5_STATE)
    outs_p = [[] for _ in range(5)]
    for l in range(depth):
        final = l == depth - 1

        z = _dense_in(xp, g_in3, w_in_b, l)
        o, hg, rw, sh, sre, sim = _mix_prompt(
            z.reshape(B, L, D_IN), hr_vecs + s5_vecs, (wup, aup, bblk, cblk, g1, g2), consts, l)
        xp = _dense_out(xp, o.reshape(B * L, D_MIX), pp, wo, gate_g3, wg, wp, g_final2, l, final)
        for acc, val in zip(outs_p, (hg, rw, sh, sre.reshape(B, S5_GROUPS, S5_P), sim.reshape(B, S5_GROUPS, S5_P))):
            acc.append(val)

        zs = _dense_in(xs, g_in3, w_in_b, l)
        o_ab, hg_s, rw_s, sh_s = _mix_sample(zs.reshape(T, BS, D_IN), hg_s, rw_s, sh_s, hr_vecs, wup, aup, ones_bd, l)
        o_c, sre_s, sim_s = _s5_sample(zs, sre_s, sim_s, s5_vecs + (bblk, cblk, g1, g2), T, BS, l)
        os_ = jnp.concatenate([o_ab.reshape(T * BS, 2 * D_HEADS), o_c], axis=1)
        xs = _dense_out(xs, os_, ps, wo, gate_g3, wg, wp, g_final2, l, final)

    y_prompt = xp.reshape(B, L, D_MODEL)
    y_sample = jnp.transpose(xs.reshape(T, BS, D_MODEL), (1, 0, 2))
    s5_shape = (depth, BS, S5_GROUPS, S5_P)
    return ((y_prompt, y_sample) + tuple(jnp.stack(a) for a in outs_p)
            + (hg_s, rw_s, sh_s, sre_s.reshape(s5_shape), sim_s.reshape(s5_shape)))
```

```python
import functools
import math

import jax
import jax.numpy as jnp
import numpy as np
from jax import lax
from jax.experimental import pallas as pl
from jax.experimental.pallas import tpu as pltpu

F32 = jnp.float32
BF16 = jnp.bfloat16

D_MODEL = 1024
D_PLE = 256
HEADS = 6
HD = 64
D_HEADS = HEADS * HD
LORA = 64
D_SHIFT = 3 * D_HEADS + 2 * LORA
S5_GROUPS = 16
S5_CH = 16
S5_P = 64
D_S5 = S5_GROUPS * S5_CH
D_S5_STATE = S5_GROUPS * S5_P
D_IN = 4 * D_HEADS + D_SHIFT + D_HEADS + 2 * D_S5
D_MIX = 2 * D_HEADS + D_S5

C_Q, C_F, C_I, C_GA = 0, D_HEADS, 2 * D_HEADS, 3 * D_HEADS
C_RW = 4 * D_HEADS
C_GB = C_RW + D_SHIFT
C_U = C_GB + D_HEADS
C_GC = C_U + D_S5

LB_FLOOR = 1e-12
NORM_EPS = 1e-6
RW_GN_EPS = 64e-5
RW_DECAY_SCALE = math.exp(-0.5)

CHUNK = 64
HG_BASE = 4
ROW_TILE = 512
PITCH = CHUNK + 8
TPU_V7X_VMEM_LIMIT = 56 * 1024 * 1024


def _dot(a, b):
    return jnp.dot(a.astype(BF16), b.astype(BF16), preferred_element_type=F32)


def _dot_nt(a, b):
    return lax.dot_general(a.astype(BF16), b.astype(BF16), (((1,), (1,)), ((), ())), preferred_element_type=F32)


def _dot_tn(a, b):
    return lax.dot_general(a.astype(BF16), b.astype(BF16), (((0,), (0,)), ((), ())), preferred_element_type=F32)


def _split3(x):
    hi = x.astype(BF16)
    r1 = x - hi.astype(F32)
    mid = r1.astype(BF16)
    lo = (r1 - mid.astype(F32)).astype(BF16)
    return hi, mid, lo


def _sel_left(m, x):
    hi, mid, lo = _split3(x)
    d = lambda p: jnp.dot(m, p, preferred_element_type=F32)
    return d(hi) + d(mid) + d(lo)


def _sel_right(x, m):
    hi = x.astype(BF16)
    lo = (x - hi.astype(F32)).astype(BF16)
    d = lambda p: jnp.dot(p, m, preferred_element_type=F32)
    return d(hi) + d(lo)


def _head_sum(x, m):
    return jnp.dot(x.astype(BF16), m, preferred_element_type=F32)


def _sigmoid(x):
    return jax.nn.sigmoid(x)


def _silu(x):
    return x * jax.nn.sigmoid(x)


def _rmsnorm(x, g):
    return x * lax.rsqrt(jnp.mean(x * x, axis=-1, keepdims=True) + NORM_EPS) * g


@functools.lru_cache(maxsize=None)
def _chunk_consts():
    C = CHUNK
    t = np.arange(C)[:, None]
    i = np.arange(C)[None, :]
    tri = (i <= t).astype(np.float32)
    masks = []
    m = C // 2
    while m >= HG_BASE:
        same = (t // (2 * m)) == (i // (2 * m))
        masks.append((same & ((t % (2 * m)) >= m) & ((i % (2 * m)) < m)).astype(np.float32))
        m //= 2
    masks.append((((t // HG_BASE) == (i // HG_BASE)) & (i <= t)).astype(np.float32))
    hmask = np.stack(masks, 0)
    rmask = np.stack([(i < t).astype(np.float32), (i <= t).astype(np.float32)], 0)
    ones_bd = np.kron(np.eye(HEADS, dtype=np.float32), np.ones((HD, HD), np.float32))
    hmask = np.tile(hmask, (1, 1, 4))
    rmask = np.tile(rmask, (1, 1, 4))
    lane_unit = np.arange(4 * HD)[None, :] // HD
    gmask = np.stack([np.broadcast_to(lane_unit == g, (C, 4 * HD)) for g in range(4)], 0).astype(np.float32)
    bdmask = np.kron(np.eye(4, dtype=np.float32), np.ones((HD, HD), np.float32))
    return tri, hmask, rmask, ones_bd, gmask, bdmask


N_LEVELS = 4


def _dense_in_kernel(x_ref, g_ref, w_ref, z_ref):
    h = _rmsnorm(x_ref[...], g_ref[...])
    z_ref[...] = jnp.dot(h.astype(BF16), w_ref[...], preferred_element_type=F32)


def _layer(a, l):
    return pl.BlockSpec((None,) + a.shape[1:], lambda i, _n=a.ndim - 1: (l,) + (0,) * _n)


def _whole(a):
    return pl.BlockSpec(a.shape, lambda i, _n=a.ndim: (0,) * _n)


def _dense_in(x, g, w, l):
    rows = x.shape[0]
    tile = min(ROW_TILE, rows)
    return pl.pallas_call(
        _dense_in_kernel,
        out_shape=jax.ShapeDtypeStruct((rows, D_IN), F32),
        grid=(rows // tile,),
        in_specs=[pl.BlockSpec((tile, D_MODEL), lambda i: (i, 0)), _layer(g, l), _layer(w, l)],
        out_specs=pl.BlockSpec((tile, D_IN), lambda i: (i, 0)),
        compiler_params=pltpu.CompilerParams(
            dimension_semantics=("arbitrary",), vmem_limit_bytes=TPU_V7X_VMEM_LIMIT),
        name="dense_in",
    )(x, g, w)


def _dense_out_kernel(x_ref, o_ref, p_ref, wo_ref, gg_ref, wg_ref, wp_ref, gf_ref, y_ref, *, final):
    x1 = x_ref[...] + jnp.dot(o_ref[...], wo_ref[...], preferred_element_type=F32)
    gate = _sigmoid(jnp.dot(_rmsnorm(x1, gg_ref[...]).astype(BF16), wg_ref[...], preferred_element_type=F32))
    x2 = x1 + jnp.dot(p_ref[...].astype(BF16), wp_ref[...], preferred_element_type=F32) * gate
    if final:
        x2 = _rmsnorm(x2, gf_ref[...])
    y_ref[...] = x2


def _dense_out(x, o, p, wo, gg, wg, wp, gf, l, final):
    rows = x.shape[0]
    tile = min(ROW_TILE, rows)
    row_spec = lambda n: pl.BlockSpec((tile, n), lambda i: (i, 0))
    p_spec = pl.BlockSpec((None, tile, D_PLE), lambda i: (l, i, 0))
    return pl.pallas_call(
        functools.partial(_dense_out_kernel, final=final),
        out_shape=jax.ShapeDtypeStruct((rows, D_MODEL), F32),
        grid=(rows // tile,),
        in_specs=[row_spec(D_MODEL), row_spec(D_MIX), p_spec, _layer(wo, l), _layer(gg, l), _layer(wg, l),
                  _layer(wp, l), _whole(gf)],
        out_specs=row_spec(D_MODEL),
        compiler_params=pltpu.CompilerParams(
            dimension_semantics=("arbitrary",), vmem_limit_bytes=TPU_V7X_VMEM_LIMIT),
        name="dense_out",
    )(x, o, p, wo, gg, wg, wp, gf)


def _s5_scan(bu_ref, sall_ref, lre, lim, s_re, s_im, nt, nb):
    lr = jnp.broadcast_to(lre, (nb, D_S5_STATE))
    li = jnp.broadcast_to(lim, (nb, D_S5_STATE))

    def step(t, carry):
        sr, si = carry
        r0 = pl.multiple_of(t * nb, nb)
        br = bu_ref[pl.ds(r0, nb), 0:D_S5_STATE]
        bi = bu_ref[pl.ds(r0, nb), D_S5_STATE:2 * D_S5_STATE]
        nr = lr * sr - li * si + br
        ni = lr * si + li * sr + bi
        sall_ref[pl.ds(r0, nb), 0:D_S5_STATE] = nr
        sall_ref[pl.ds(r0, nb), D_S5_STATE:2 * D_S5_STATE] = ni
        return nr, ni

    return lax.fori_loop(0, nt, step, (s_re, s_im))


def _s5_head(u, sall_ref, cblk_ref, d_ref, g1_ref, g2_ref):
    y = jnp.dot(sall_ref[...].astype(BF16), cblk_ref[...], preferred_element_type=F32) + d_ref[...] * u
    y = jax.nn.gelu(y)
    yb = y.astype(BF16)
    return jnp.dot(yb, g1_ref[...], preferred_element_type=F32) * _sigmoid(
        jnp.dot(yb, g2_ref[...], preferred_element_type=F32))


PACK = 4
PW = PACK * HD
SEQ_PAIRS = 2
S5_STEPS = 16


def _interleave(required, background=()):
    live = list(required)
    extra = list(background)
    while live:
        for entry in list(live) + list(extra):
            gen, k = entry
            for _ in range(k):
                try:
                    next(gen)
                except StopIteration:
                    (live if entry in live else extra).remove(entry)
                    break


def _mix_prompt_packed_kernel(z_ref, lb_ref, hgn_ref, mu_ref, w0_ref, a0_ref, kk_ref, ka_ref, rk_ref, gng_ref,
                              gnb_ref, lre_ref, lim_ref, d_ref, wup_ref, aup_ref, bblk_ref, cblk_ref, g1_ref, g2_ref,
                              tri_ref, hmask_ref, rmask_ref, ones_ref, gmask_ref, bdmask_ref,
                              o_ref, hg_ref, rw_ref, sh_ref, sre_ref, sim_ref,
                              hg_bd0, hg_bd1, hg_bd2, rw_bd0, rw_bd1, rw_bd2, upad, up, bu, sall, ypad):
    nb = z_ref.shape[0]
    C = CHUNK
    n_pairs = D_HEADS // 128
    hg_bd = (hg_bd0, hg_bd1, hg_bd2)
    rw_bd = (rw_bd0, rw_bd1, rw_bd2)
    step = pl.program_id(0)

    @pl.when(step == 0)
    def _():
        for ref in hg_bd + rw_bd:
            ref[...] = jnp.zeros(ref.shape, F32)
        sh_ref[...] = jnp.zeros(sh_ref.shape, F32)
        sre_ref[...] = jnp.zeros(sre_ref.shape, F32)
        sim_ref[...] = jnp.zeros(sim_ref.shape, F32)

    ones_bd = ones_ref[...]
    bdmask = bdmask_ref[...]
    row_is0 = lax.broadcasted_iota(jnp.int32, (C, D_SHIFT), 0) == 0
    low_half = lax.broadcasted_iota(jnp.int32, (C, D_HEADS), 0) % 8 < HG_BASE

    def bd(x):
        m = x.shape[1] // PW
        blocks = []
        for g in range(PACK):
            mask = gmask_ref[g]
            if m > 1:
                mask = jnp.concatenate([mask] * m, axis=1)
            blocks.append(x * mask)
        return jnp.concatenate(blocks, axis=0)

    def prep_stages(b, p):
        q = z_ref[b, :, C_Q:C_Q + D_HEADS]
        f = z_ref[b, :, C_F:C_F + D_HEADS]
        lbs = jnp.maximum(lb_ref[...], LB_FLOOR)
        logf = jnp.log(lbs + (1.0 - lbs) * _sigmoid(f))
        kf = (1.0 - lbs) * _sigmoid(-f)
        qf = _silu(q)
        yield
        cum = _sel_left(tri_ref[...], logf)
        p["qin"] = (qf * jnp.exp(cum)).astype(BF16)
        p["kend"] = (kf * jnp.exp(cum[C - 1:C, :] - cum)).astype(BF16)
        yield
        for l in range(N_LEVELS):
            m = C >> (l + 1)
            ref = jnp.concatenate([jnp.broadcast_to(cum[q0 + m - 1:q0 + m, :], (2 * m, D_HEADS))
                                   for q0 in range(0, C, 2 * m)], axis=0)
            e = jnp.exp(-jnp.abs(cum - ref))
            p["qe%d" % l] = (qf * e).astype(BF16)
            p["ke%d" % l] = (kf * e).astype(BF16)
            yield
        first = jnp.concatenate([jnp.broadcast_to(cum[q0:q0 + 1, :], (8, D_HEADS)) for q0 in range(0, C, 8)], axis=0)
        second = jnp.concatenate([jnp.broadcast_to(cum[q0 + HG_BASE:q0 + HG_BASE + 1, :], (8, D_HEADS))
                                  for q0 in range(0, C, 8)], axis=0)
        dq = cum - jnp.where(low_half, first, second)
        p["qe%d" % N_LEVELS] = (qf * jnp.exp(dq)).astype(BF16)
        p["ke%d" % N_LEVELS] = (kf * jnp.exp(-dq)).astype(BF16)
        p["gcol"] = jnp.exp(cum[C - 8:C, :].T)[:, 7:8]
        p["ib"] = z_ref[b, :, C_I:C_I + D_HEADS].astype(BF16)
        yield

        c = z_ref[b, :, C_RW:C_RW + D_SHIFT]
        prev = jnp.where(row_is0, sh_ref[pl.ds(b, 1), :], pltpu.roll(c, 1, axis=0))
        p["last"] = c[C - 1:C, :]
        cs = c + mu_ref[...] * (prev - c)
        r = cs[:, 0:D_HEADS]
        k = cs[:, D_HEADS:2 * D_HEADS]
        v = cs[:, 2 * D_HEADS:3 * D_HEADS]
        wd = cs[:, 3 * D_HEADS:3 * D_HEADS + LORA]
        ad = cs[:, 3 * D_HEADS + LORA:D_SHIFT]
        logw = -RW_DECAY_SCALE * _sigmoid(w0_ref[...] + _dot(jnp.tanh(wd), wup_ref[...]))
        a = _sigmoid(a0_ref[...] + _dot(ad, aup_ref[...]))
        yield
        kk = k * kk_ref[...]
        kk = kk * lax.rsqrt(jnp.maximum(_head_sum(kk * kk, ones_bd), 1e-24))
        kt = k * (1.0 + (a - 1.0) * ka_ref[...])
        beta = kk * a
        p["bonus"] = _head_sum(r * kt * rk_ref[...], ones_bd)
        cl = _sel_left(tri_ref[...], logw)
        yield
        p["aq"] = kk * jnp.exp(cl - logw)
        p["rq"] = r * jnp.exp(cl)
        ecl = jnp.exp(-cl)
        p["kd"] = (kt * ecl).astype(BF16)
        p["bd"] = (beta * ecl).astype(BF16)
        eend = jnp.exp(cl[C - 1:C, :] - cl)
        p["ktl"] = kt * eend
        p["btl"] = beta * eend
        p["gam"] = jnp.exp(cl[C - 1:C, :])
        p["v"] = v

    def chain_stages(i, ps):
        b0 = 2 * SEQ_PAIRS * i
        pairs = range(SEQ_PAIRS * n_pairs)
        pj = [u % n_pairs for u in pairs]
        pq = [2 * (u // n_pairs) for u in pairs]
        tiles = [slice(pj[u] * 128, (pj[u] + 1) * 128) for u in pairs]
        slot = [SEQ_PAIRS * i + u // n_pairs for u in pairs]

        def x4(name, rows=False):
            if rows:
                return [jnp.concatenate([ps[pq[u]][name][tiles[u], :], ps[pq[u] + 1][name][tiles[u], :]], axis=0)
                        for u in pairs]
            return [jnp.concatenate([ps[pq[u]][name][:, tiles[u]], ps[pq[u] + 1][name][:, tiles[u]]], axis=1)
                    for u in pairs]

        s_hg = [hg_bd[pj[u]][slot[u]] for u in pairs]
        s_rw = [rw_bd[pj[u]][slot[u]] for u in pairs]
        aq4, rq4, v4, btl4, ktl4 = x4("aq"), x4("rq"), x4("v"), x4("btl"), x4("ktl")
        bd4, kd4 = x4("bd"), x4("kd")
        ar = [jnp.concatenate([aq4[j], rq4[j]], axis=0).astype(BF16) for j in pairs]
        zb = [_dot_nt(ar[j], bd(bd4[j])) for j in pairs]
        zk = [_dot_nt(ar[j], bd(kd4[j])) for j in pairs]
        yield
        mb = [zb[j][0:C] * rmask_ref[0] for j in pairs]
        nbm = [zb[j][C:2 * C] * rmask_ref[1] for j in pairs]
        mk = [zk[j][0:C] * rmask_ref[0] for j in pairs]
        nk = [zk[j][C:2 * C] * rmask_ref[1] for j in pairs]
        bdv = [bd(v4[j].astype(BF16)) for j in pairs]
        x0 = [jnp.concatenate([aq4[j], _dot(mk[j], bdv[j])], axis=1) for j in pairs]
        yield
        bm = [-m for m in mb]
        pw = [_dot(mb[j], bd(mb[j].astype(BF16))) for j in pairs]
        att = [None for _ in pairs]
        level = 0
        n = 2
        while 2 * n < C:
            yield
            both = [_dot(jnp.concatenate([bm[j], pw[j]], axis=0), bd(pw[j].astype(BF16))) for j in pairs]
            bm = [bm[j] + pw[j] + both[j][0:C] for j in pairs]
            pw = [both[j][C:2 * C] for j in pairs]
            n *= 2
            qe4, ke4 = x4("qe%d" % level), x4("ke%d" % level)
            for j in pairs:
                term = hmask_ref[level] * _dot_nt(qe4[j], bd(ke4[j]))
                att[j] = term if att[j] is None else att[j] + term
            level += 1
        yield
        bm = [bm[j] + pw[j] + _dot(bm[j], bd(pw[j].astype(BF16))) for j in pairs]
        while level <= N_LEVELS:
            qe4, ke4 = x4("qe%d" % level), x4("ke%d" % level)
            for j in pairs:
                term = hmask_ref[level] * _dot_nt(qe4[j], bd(ke4[j]))
                att[j] = term if att[j] is None else att[j] + term
            level += 1
        yield
        ib4, qin4, kend4 = x4("ib"), x4("qin"), x4("kend")
        o4 = [_dot(att[j], bd(ib4[j])) + _dot(qin4[j], s_hg[j]) for j in pairs]
        gcol4 = x4("gcol", rows=True)
        new_hg = [gcol4[j] * s_hg[j] + bdmask * _dot_tn(kend4[j], ib4[j]) for j in pairs]
        x = [x0[j] + _dot(bm[j], bd(x0[j].astype(BF16))) for j in pairs]
        yield
        bdx = [bd(x[j].astype(BF16)) for j in pairs]
        w = [bdmask * _dot_tn(btl4[j], x[j][:, 0:PW]) for j in pairs]
        gt = [bdmask * _dot_tn(jnp.concatenate([v4[j], x[j][:, PW:2 * PW]], axis=0),
                               jnp.concatenate([ktl4[j], -btl4[j]], axis=0)) for j in pairs]
        nx = [_dot(nbm[j], bdx[j]) for j in pairs]
        yield
        y4 = [_dot_nt(rq4[j] - nx[j][:, 0:PW], s_rw[j]) + _dot(nk[j], bdv[j]) - nx[j][:, PW:2 * PW] for j in pairs]
        gam4 = x4("gam")
        new_rw = [s_rw[j] * gam4[j] - _dot_nt(s_rw[j], w[j]) + gt[j] for j in pairs]
        for j in pairs:
            hg_bd[pj[j]][slot[j]] = new_hg[j]
            rw_bd[pj[j]][slot[j]] = new_rw[j]
        yield
        o_tiles = [[o4[(u // 2) * n_pairs + j][:, (u % 2) * 128:(u % 2 + 1) * 128] for j in range(n_pairs)]
                   for u in range(2 * SEQ_PAIRS)]
        y_tiles = [[y4[(u // 2) * n_pairs + j][:, (u % 2) * 128:(u % 2 + 1) * 128] for j in range(n_pairs)]
                   for u in range(2 * SEQ_PAIRS)]
        for u in range(2 * SEQ_PAIRS):
            b = b0 + u
            o = jnp.concatenate(o_tiles[u], axis=1)
            ssq = _head_sum(o * o, ones_bd)
            o = o * lax.rsqrt(ssq * (1.0 / HD) + NORM_EPS) * hgn_ref[...]
            o_ref[b, :, 0:D_HEADS] = (o * _silu(z_ref[b, :, C_GA:C_GA + D_HEADS])).astype(BF16)
            y = jnp.concatenate(y_tiles[u], axis=1)
            yc = y - _head_sum(y, ones_bd) * (1.0 / HD)
            var = _head_sum(yc * yc, ones_bd) * (1.0 / HD)
            yn = yc * lax.rsqrt(var + RW_GN_EPS) * gng_ref[...] + gnb_ref[...] + ps[u]["bonus"] * ps[u]["v"]
            o_ref[b, :, D_HEADS:2 * D_HEADS] = (yn * _silu(z_ref[b, :, C_GB:C_GB + D_HEADS])).astype(BF16)
            sh_ref[pl.ds(b, 1), :] = ps[u]["last"]
            yield

    n_lane = D_S5 // 128
    n_groups = nb // (2 * SEQ_PAIRS)
    s5_blocks = C // S5_STEPS // n_groups
    blk_rows = S5_STEPS * nb

    def s5_stages(i, state):
        lr = jnp.broadcast_to(lre_ref[...], (nb, D_S5_STATE))
        li = jnp.broadcast_to(lim_ref[...], (nb, D_S5_STATE))
        sr, si = state
        for k in range(s5_blocks):
            t0 = (i * s5_blocks + k) * S5_STEPS
            row0 = pl.multiple_of(t0 * nb, blk_rows)
            for tt in range(S5_STEPS):
                for j in range(n_lane):
                    up[pl.ds(pl.multiple_of(row0 + tt * nb, nb), nb), j * 128:(j + 1) * 128] = (
                        upad[j, pl.ds(t0 + tt, nb, stride=PITCH), :])
            u = up[pl.ds(row0, blk_rows), :]
            bu[pl.ds(row0, blk_rows), :] = jnp.dot(u.astype(BF16), bblk_ref[...], preferred_element_type=F32)
            yield
            for tt in range(S5_STEPS):
                r0 = pl.multiple_of(row0 + tt * nb, nb)
                br = bu[pl.ds(r0, nb), 0:D_S5_STATE]
                bi = bu[pl.ds(r0, nb), D_S5_STATE:2 * D_S5_STATE]
                sr, si = lr * sr - li * si + br, lr * si + li * sr + bi
                sall[pl.ds(r0, nb), 0:D_S5_STATE] = sr
                sall[pl.ds(r0, nb), D_S5_STATE:2 * D_S5_STATE] = si
                if tt % 4 == 3:
                    yield
            y = jnp.dot(sall[pl.ds(row0, blk_rows), :].astype(BF16), cblk_ref[...],
                        preferred_element_type=F32) + d_ref[...] * u
            yb = jax.nn.gelu(y).astype(BF16)
            yg = jnp.dot(yb, g1_ref[...], preferred_element_type=F32) * _sigmoid(
                jnp.dot(yb, g2_ref[...], preferred_element_type=F32))
            for tt in range(S5_STEPS):
                for j in range(n_lane):
                    ypad[j, pl.ds(t0 + tt, nb, stride=PITCH), :] = yg[tt * nb:(tt + 1) * nb, j * 128:(j + 1) * 128]
            yield
        state[0], state[1] = sr, si

    def prep_group(i, ps):
        for q in range(2 * SEQ_PAIRS):
            yield from prep_stages(2 * SEQ_PAIRS * i + q, ps[q])

    for b in range(nb):
        for j in range(n_lane):
            upad[j, b * PITCH:b * PITCH + C, :] = z_ref[b, :, C_U + j * 128:C_U + (j + 1) * 128]

    def group_body(i, carry):
        ps = [{} for _ in range(2 * SEQ_PAIRS)]
        state = list(carry)
        s5 = (s5_stages(i, state), 1)
        _interleave([(prep_group(i, ps), 1)])
        _interleave([(chain_stages(i, ps), 1)], background=[s5])
        _interleave([s5])
        return tuple(state)

    s_re, s_im = lax.fori_loop(0, n_groups, group_body, (sre_ref[...], sim_ref[...]))
    sre_ref[...] = s_re
    sim_ref[...] = s_im
    for b in range(nb):
        for j in range(n_lane):
            gate = _silu(z_ref[b, :, C_GC + j * 128:C_GC + (j + 1) * 128])
            o_ref[b, :, 2 * D_HEADS + j * 128:2 * D_HEADS + (j + 1) * 128] = (
                ypad[j, b * PITCH:b * PITCH + C, :] * gate).astype(BF16)

    @pl.when(step == pl.num_programs(0) - 1)
    def _():
        for i in range(nb // 2):
            for j in range(n_pairs):
                for g in range(PACK):
                    b = 2 * i + g // 2
                    h = 2 * j + g % 2
                    blk = slice(g * HD, (g + 1) * HD)
                    hg_ref[b, h] = hg_bd[j][i, blk, :][:, blk]
                    rw_ref[b, h] = rw_bd[j][i, blk, :][:, blk]


def _mix_prompt(z, vecs, mats, consts, l):
    B, L, _ = z.shape
    nt = L // CHUNK
    out_shape = (
        jax.ShapeDtypeStruct((B, L, D_MIX), BF16),
        jax.ShapeDtypeStruct((B, HEADS, HD, HD), F32),
        jax.ShapeDtypeStruct((B, HEADS, HD, HD), F32),
        jax.ShapeDtypeStruct((B, D_SHIFT), F32),
        jax.ShapeDtypeStruct((B, D_S5_STATE), F32),
        jax.ShapeDtypeStruct((B, D_S5_STATE), F32),
    )
    out_specs = (
        pl.BlockSpec((B, CHUNK, D_MIX), lambda i: (0, i, 0)),
        pl.BlockSpec((B, HEADS, HD, HD), lambda i: (0, 0, 0, 0)),
        pl.BlockSpec((B, HEADS, HD, HD), lambda i: (0, 0, 0, 0)),
        pl.BlockSpec((B, D_SHIFT), lambda i: (0, 0)),
        pl.BlockSpec((B, D_S5_STATE), lambda i: (0, 0)),
        pl.BlockSpec((B, D_S5_STATE), lambda i: (0, 0)),
    )
    args = (z,) + tuple(vecs) + tuple(mats) + tuple(consts)
    in_specs = ([pl.BlockSpec((B, CHUNK, D_IN), lambda i: (0, i, 0))]
                + [_layer(a, l) for a in tuple(vecs) + tuple(mats)] + [_whole(a) for a in consts])
    rows = B * CHUNK
    assert B % (2 * SEQ_PAIRS) == 0 and D_HEADS // 128 == 3
    scratch = [pltpu.VMEM((B // 2, PW, PW), F32) for _ in range(6)] + [
        pltpu.VMEM((D_S5 // 128, B * PITCH, 128), F32),
        pltpu.VMEM((rows, D_S5), F32),
        pltpu.VMEM((rows, 2 * D_S5_STATE), F32),
        pltpu.VMEM((rows, 2 * D_S5_STATE), F32),
        pltpu.VMEM((D_S5 // 128, B * PITCH, 128), F32),
    ]
    return pl.pallas_call(
        _mix_prompt_packed_kernel,
        out_shape=out_shape,
        grid=(nt,),
        in_specs=in_specs,
        out_specs=out_specs,
        scratch_shapes=scratch,
        compiler_params=pltpu.CompilerParams(
            dimension_semantics=("arbitrary",), vmem_limit_bytes=TPU_V7X_VMEM_LIMIT),
        name="mix_prompt",
    )(*args)


SAMPLE_BLOCK = 16
SAMPLE_UNROLL = 4


def _head_sums(terms, ones_bd):
    nb = terms[0].shape[0]
    s = _sel_right(jnp.concatenate(terms, axis=0), ones_bd)
    return [s[i * nb:(i + 1) * nb] for i in range(len(terms))]


def _mix_sample_chunk_kernel(z_ref, hg_in, rw_in, sh_in, lb_ref, hgn_ref, mu_ref, w0_ref, a0_ref, kk_ref, ka_ref,
                             rk_ref, gng_ref, gnb_ref, wup_ref, aup_ref, ones_ref,
                             o_ref, hg_ref, rw_ref, sh_ref,
                             lhs_hg, lhs_rw, base_hg, base_rw, upd_hg, upd_rw, gam_hg, gam_rw, tmp):
    T, nb, _ = z_ref.shape
    ones_bd = ones_ref[...]
    lbs = jnp.maximum(lb_ref[...], LB_FLOOR)
    heads = range(HEADS)
    hsl = [slice(h * HD, (h + 1) * HD) for h in heads]

    fg, kf, qf, iv, kk, dec, beta, kt, r, v = ([] for _ in range(10))
    for t in range(T):
        f = z_ref[t, :, C_F:C_F + D_HEADS]
        fg.append(lbs + (1.0 - lbs) * _sigmoid(f))
        kf.append((1.0 - lbs) * _sigmoid(-f))
        qf.append(_silu(z_ref[t, :, C_Q:C_Q + D_HEADS]))
        iv.append(z_ref[t, :, C_I:C_I + D_HEADS])
        c = z_ref[t, :, C_RW:C_RW + D_SHIFT]
        prev = sh_in[...] if t == 0 else z_ref[t - 1, :, C_RW:C_RW + D_SHIFT]
        cs = c + mu_ref[...] * (prev - c)
        k_t = cs[:, D_HEADS:2 * D_HEADS]
        wd = cs[:, 3 * D_HEADS:3 * D_HEADS + LORA]
        ad = cs[:, 3 * D_HEADS + LORA:D_SHIFT]
        logw = -RW_DECAY_SCALE * _sigmoid(w0_ref[...] + _dot(jnp.tanh(wd), wup_ref[...]))
        a = _sigmoid(a0_ref[...] + _dot(ad, aup_ref[...]))
        kk_t = k_t * kk_ref[...]
        kk_t = kk_t * lax.rsqrt(jnp.maximum(_sel_right(kk_t * kk_t, ones_bd), 1e-24))
        kk.append(kk_t)
        dec.append(jnp.exp(logw))
        beta.append(kk_t * a)
        kt.append(k_t * (1.0 + (a - 1.0) * ka_ref[...]))
        r.append(cs[:, 0:D_HEADS])
        v.append(cs[:, 2 * D_HEADS:3 * D_HEADS])
    sh_ref[...] = z_ref[T - 1, :, C_RW:C_RW + D_SHIFT]

    def span(x, lo, hi):
        out = None
        for i in range(lo, hi + 1):
            out = x[i] if out is None else out * x[i]
        return out

    def scaled(x, p):
        return x if p is None else x * p

    for t in range(T):
        lhs_hg[t] = qf[t] * span(fg, 0, t)
        lhs_rw[t] = scaled(kk[t], span(dec, 0, t - 1))
        lhs_rw[T + t] = r[t] * span(dec, 0, t)
    gam_hg[...] = span(fg, 0, T - 1)
    tmp[...] = jnp.zeros(tmp.shape, F32)

    unroll = range(SAMPLE_UNROLL)

    def state_queries(i, carry):
        bs = [i * SAMPLE_UNROLL + u for u in unroll]
        for u in unroll:
            for t in range(T):
                tmp[5 * u, t:t + 1, :] = lhs_hg[t, pl.ds(bs[u], 1), :]
                tmp[5 * u + 1, t:t + 1, :] = lhs_rw[t, pl.ds(bs[u], 1), :]
                tmp[5 * u + 1, T + t:T + t + 1, :] = lhs_rw[T + t, pl.ds(bs[u], 1), :]
        qh = [tmp[5 * u] for u in unroll]
        ar = [tmp[5 * u + 1] for u in unroll]
        bo = [jnp.concatenate([_dot(qh[u][:, hsl[h]], hg_in[bs[u], h]) for h in heads], axis=1) for u in unroll]
        br = [jnp.concatenate([_dot_nt(ar[u][:, hsl[h]], rw_in[bs[u], h]) for h in heads], axis=1) for u in unroll]
        for u in unroll:
            for t in range(T):
                base_hg[t, pl.ds(bs[u], 1), :] = bo[u][t:t + 1, :]
                base_rw[t, pl.ds(bs[u], 1), :] = br[u][t:t + 1, :]
                base_rw[T + t, pl.ds(bs[u], 1), :] = br[u][T + t:T + t + 1, :]
        return carry

    lax.fori_loop(0, nb // SAMPLE_UNROLL, state_queries, 0)

    pairs = [(t, s) for t in range(T) for s in range(t + 1)]
    att = dict(zip(pairs, _head_sums([scaled(qf[t] * kf[s], span(fg, s + 1, t)) for t, s in pairs], ones_bd)))
    for t in range(T):
        o = base_hg[t]
        for s in range(t + 1):
            o = o + att[(t, s)] * iv[s]
        ssq = _sel_right(o * o, ones_bd)
        o = o * lax.rsqrt(ssq * (1.0 / HD) + NORM_EPS) * hgn_ref[...]
        o_ref[t, :, 0:D_HEADS] = (o * _silu(z_ref[t, :, C_GA:C_GA + D_HEADS])).astype(BF16)
        upd_hg[t] = scaled(kf[t], span(fg, t + 1, T - 1))
        upd_hg[T + t] = iv[t]

    strict = [(t, s) for t in range(T) for s in range(t)]
    terms = ([scaled(kk[t] * beta[s], span(dec, s + 1, t - 1)) for t, s in strict]
             + [scaled(kk[t] * kt[s], span(dec, s + 1, t - 1)) for t, s in strict]
             + [scaled(r[t] * beta[s], span(dec, s + 1, t)) for t, s in pairs]
             + [scaled(r[t] * kt[s], span(dec, s + 1, t)) for t, s in pairs]
             + [r[t] * kt[t] * rk_ref[...] for t in range(T)])
    sums = _head_sums(terms, ones_bd)
    ns, npairs = len(strict), len(pairs)
    m_b = dict(zip(strict, sums[0:ns]))
    m_k = dict(zip(strict, sums[ns:2 * ns]))
    n_b = dict(zip(pairs, sums[2 * ns:2 * ns + npairs]))
    n_k = dict(zip(pairs, sums[2 * ns + npairs:2 * ns + 2 * npairs]))
    bonus = sums[2 * ns + 2 * npairs:]
    u = []
    for t in range(T):
        u_t = base_rw[t]
        for s in range(t):
            u_t = u_t + m_k[(t, s)] * v[s] - m_b[(t, s)] * u[s]
        u.append(u_t)
    for t in range(T):
        y = base_rw[T + t]
        for s in range(t + 1):
            y = y + n_k[(t, s)] * v[s] - n_b[(t, s)] * u[s]
        yc = y - _sel_right(y, ones_bd) * (1.0 / HD)
        var = _sel_right(yc * yc, ones_bd) * (1.0 / HD)
        yn = yc * lax.rsqrt(var + RW_GN_EPS) * gng_ref[...] + gnb_ref[...] + bonus[t] * v[t]
        o_ref[t, :, D_HEADS:2 * D_HEADS] = (yn * _silu(z_ref[t, :, C_GB:C_GB + D_HEADS])).astype(BF16)
        tail = span(dec, t + 1, T - 1)
        upd_rw[0, t] = v[t]
        upd_rw[0, T + t] = u[t]
        upd_rw[1, t] = scaled(kt[t], tail)
        upd_rw[1, T + t] = -scaled(beta[t], tail)
    gam_rw[...] = span(dec, 0, T - 1)

    is_key_row = lax.broadcasted_iota(jnp.int32, (8, D_HEADS), 0) < T

    def state_updates(i, carry):
        bs = [i * SAMPLE_UNROLL + u for u in unroll]
        for u in unroll:
            for t in range(T):
                tmp[5 * u, t:t + 1, :] = upd_hg[t, pl.ds(bs[u], 1), :]
                tmp[5 * u + 2, t:t + 1, :] = upd_hg[T + t, pl.ds(bs[u], 1), :]
            tmp[5 * u, T:T + 1, :] = gam_hg[pl.ds(bs[u], 1), :]
            for j in range(2 * T):
                tmp[5 * u + 3, j:j + 1, :] = upd_rw[0, j, pl.ds(bs[u], 1), :]
                tmp[5 * u + 4, j:j + 1, :] = upd_rw[1, j, pl.ds(bs[u], 1), :]
        kg = [tmp[5 * u] for u in unroll]
        gcol = [k_u.T for k_u in kg]
        kq = [jnp.where(is_key_row, k_u, 0.0) for k_u in kg]
        ii = [tmp[5 * u + 2] for u in unroll]
        vu = [tmp[5 * u + 3] for u in unroll]
        kb = [tmp[5 * u + 4] for u in unroll]
        g_rw = [gam_rw[pl.ds(bs[u], 1), :] for u in unroll]
        new_hg = [[gcol[u][hsl[h], T:T + 1] * hg_in[bs[u], h] + _dot_tn(kq[u][:, hsl[h]], ii[u][:, hsl[h]])
                   for h in heads] for u in unroll]
        new_rw = [[rw_in[bs[u], h] * g_rw[u][:, hsl[h]] + _dot_tn(vu[u][:, hsl[h]], kb[u][:, hsl[h]])
                   for h in heads] for u in unroll]
        for u in unroll:
            for h in heads:
                hg_ref[bs[u], h] = new_hg[u][h]
                rw_ref[bs[u], h] = new_rw[u][h]
        return carry

    lax.fori_loop(0, nb // SAMPLE_UNROLL, state_updates, 0)


def _mix_sample_entry(*refs, n_carried):
    _mix_sample_chunk_kernel(*(refs[:4] + refs[4 + n_carried:]))


def _mix_sample(z, hg, rw, sh, carried, vecs, wup, aup, ones_bd, l):
    T, B, _ = z.shape
    assert 2 * T == 8, "the row staging buffers hold the 2T rows of one sequence in one 8-sublane tile"
    nb = SAMPLE_BLOCK
    st4 = pl.BlockSpec((None, nb, HEADS, HD, HD), lambda i: (l, i, 0, 0, 0))
    sh_spec = pl.BlockSpec((None, nb, D_SHIFT), lambda i: (l, i, 0))
    stacked = tuple(vecs) + (wup, aup)
    rest = stacked + (ones_bd,)
    n_carried = len(carried)
    return pl.pallas_call(
        functools.partial(_mix_sample_entry, n_carried=n_carried),
        out_shape=(
            jax.ShapeDtypeStruct((T, B, 2 * D_HEADS), BF16),
            jax.ShapeDtypeStruct(hg.shape, F32),
            jax.ShapeDtypeStruct(rw.shape, F32),
            jax.ShapeDtypeStruct(sh.shape, F32),
        ),
        grid=(B // nb,),
        in_specs=([pl.BlockSpec((T, nb, D_IN), lambda i: (0, i, 0)), st4, st4, sh_spec]
                  + [pl.BlockSpec(memory_space=pl.ANY)] * n_carried
                  + [_layer(a, l) for a in stacked] + [_whole(ones_bd)]),
        out_specs=(pl.BlockSpec((T, nb, 2 * D_HEADS), lambda i: (0, i, 0)), st4, st4, sh_spec),
        input_output_aliases={4 + k: 1 + k for k in range(n_carried)},
        scratch_shapes=[
            pltpu.VMEM((T, nb, D_HEADS), F32),
            pltpu.VMEM((2 * T, nb, D_HEADS), F32),
            pltpu.VMEM((T, nb, D_HEADS), F32),
            pltpu.VMEM((2 * T, nb, D_HEADS), F32),
            pltpu.VMEM((2 * T, nb, D_HEADS), F32),
            pltpu.VMEM((2, 2 * T, nb, D_HEADS), F32),
            pltpu.VMEM((nb, D_HEADS), F32),
            pltpu.VMEM((nb, D_HEADS), F32),
            pltpu.VMEM((5 * SAMPLE_UNROLL, 8, D_HEADS), F32),
        ],
        compiler_params=pltpu.CompilerParams(
            dimension_semantics=("arbitrary",), vmem_limit_bytes=TPU_V7X_VMEM_LIMIT),
        name="mix_sample",
    )(z, hg, rw, sh, *carried, *rest)


def _s5_sample_kernel(z_ref, sre_in, sim_in, lre_ref, lim_ref, d_ref, bblk_ref, cblk_ref, g1_ref, g2_ref,
                      o_ref, sre_ref, sim_ref, bu, sall, *, nt, nb):
    u = z_ref[:, C_U:C_U + D_S5]
    bu[...] = jnp.dot(u.astype(BF16), bblk_ref[...], preferred_element_type=F32)
    s_re, s_im = _s5_scan(bu, sall, lre_ref[...], lim_ref[...], sre_in[...], sim_in[...], nt, nb)
    sre_ref[...] = s_re
    sim_ref[...] = s_im
    yg = _s5_head(u, sall, cblk_ref, d_ref, g1_ref, g2_ref)
    o_ref[...] = (yg * _silu(z_ref[:, C_GC:C_GC + D_S5])).astype(BF16)


def _s5_sample(z2d, sre, sim, stacked, nt, nb, l):
    rows = z2d.shape[0]
    return pl.pallas_call(
        functools.partial(_s5_sample_kernel, nt=nt, nb=nb),
        out_shape=(
            jax.ShapeDtypeStruct((rows, D_S5), BF16),
            jax.ShapeDtypeStruct(sre.shape, F32),
            jax.ShapeDtypeStruct(sim.shape, F32),
        ),
        grid=(1,),
        in_specs=[_whole(z2d), _layer(sre, l), _layer(sim, l)] + [_layer(a, l) for a in stacked],
        out_specs=(pl.BlockSpec((rows, D_S5), lambda i: (0, 0)), _layer(sre, l), _layer(sim, l)),
        input_output_aliases={1: 1, 2: 2},
        scratch_shapes=[pltpu.VMEM((rows, 2 * D_S5_STATE), F32), pltpu.VMEM((rows, 2 * D_S5_STATE), F32)],
        compiler_params=pltpu.CompilerParams(
            dimension_semantics=("arbitrary",), vmem_limit_bytes=TPU_V7X_VMEM_LIMIT),
        name="s5_sample",
    )(z2d, sre, sim, *stacked)


def _prep_kernel(lbraw_ref, are_ref, aim_ref, ldt_ref, bre_ref, bim_ref, lb_ref, lre_ref, lim_ref, bbre_ref, bbim_ref):
    raw = lbraw_ref[...]
    e = jnp.exp(raw - jnp.max(raw, axis=0, keepdims=True))
    sm = e / jnp.sum(e, axis=0, keepdims=True)
    depth = raw.shape[0]
    acc = jnp.zeros_like(sm[0:1])
    for l in range(depth):
        acc = acc + sm[l:l + 1]
        lb_ref[l:l + 1, :] = acc - sm[0:1]
    a_re = are_ref[...]
    a_im = aim_ref[...]
    dt = jnp.exp(ldt_ref[...])
    mag = jnp.exp(dt * a_re)
    lam_re = mag * jnp.cos(dt * a_im)
    lam_im = mag * jnp.sin(dt * a_im)
    den = a_re * a_re + a_im * a_im
    xr = lam_re - 1.0
    f_re = (xr * a_re + lam_im * a_im) / den
    f_im = (lam_im * a_re - xr * a_im) / den
    lre_ref[...] = lam_re
    lim_ref[...] = lam_im
    b_re = bre_ref[...]
    b_im = bim_ref[...]
    bbre_ref[...] = f_re * b_re - f_im * b_im
    bbim_ref[...] = f_re * b_im + f_im * b_re


def _prep(hg_lb_raw, s5_a_re, s5_a_im, s5_log_dt, s5_b_re, s5_b_im):
    depth = hg_lb_raw.shape[0]
    n = depth * S5_GROUPS * S5_P
    col = lambda a: a.reshape(n, 1)
    ldt = jnp.broadcast_to(s5_log_dt[:, :, None], (depth, S5_GROUPS, S5_P))
    lb, lre, lim, bbre, bbim = pl.pallas_call(
        _prep_kernel,
        out_shape=(
            jax.ShapeDtypeStruct((depth, D_HEADS), F32),
            jax.ShapeDtypeStruct((n, 1), F32),
            jax.ShapeDtypeStruct((n, 1), F32),
            jax.ShapeDtypeStruct((n, S5_CH), F32),
            jax.ShapeDtypeStruct((n, S5_CH), F32),
        ),
        name="param_prep",
    )(hg_lb_raw, col(s5_a_re), col(s5_a_im), col(ldt), s5_b_re.reshape(n, S5_CH), s5_b_im.reshape(n, S5_CH))
    shape3 = (depth, S5_GROUPS, S5_P)
    return (lb, lre.reshape(depth, 1, D_S5_STATE), lim.reshape(depth, 1, D_S5_STATE),
            bbre.reshape(shape3 + (S5_CH,)), bbim.reshape(shape3 + (S5_CH,)))


def kernel(x_prompt, x_sample, state_hgrn, state_rwkv, state_rwkv_shift, state_s5_re, state_s5_im, p_prompt, p_sample, g_in, w_in, hg_lb_raw, hg_norm_g, rw_mu, rw_w0, rw_w_up, rw_a0, rw_a_up, rw_k_k, rw_k_a, rw_r_k, rw_gn_g, rw_gn_b, s5_a_re, s5_a_im, s5_log_dt, s5_b_re, s5_b_im, s5_c_re, s5_c_im, s5_d, s5_glu_w1, s5_glu_w2, w_out, ple_w_proj, ple_gate_g, ple_w_gate, g_final):
    depth = w_in.shape[0]
    B, L, _ = x_prompt.shape
    BS, T, _ = x_sample.shape
    tri, hmask, rmask, ones_np, gmask, bdmask = _chunk_consts()
    ones_bd = jnp.asarray(ones_np, BF16)
    consts = (jnp.asarray(tri, BF16), jnp.asarray(hmask, F32), jnp.asarray(rmask, F32), ones_bd,
              jnp.asarray(gmask, BF16), jnp.asarray(bdmask, F32))

    lb_all, lam_re, lam_im, bb_re, bb_im = _prep(hg_lb_raw, s5_a_re, s5_a_im, s5_log_dt, s5_b_re, s5_b_im)
    g_final2 = g_final.reshape(1, -1)

    xp = x_prompt.reshape(B * L, D_MODEL)
    xs = jnp.transpose(x_sample, (1, 0, 2)).reshape(T * BS, D_MODEL)
    pp = p_prompt.reshape(depth, B * L, D_PLE)
    ps = jnp.transpose(p_sample, (0, 2, 1, 3)).reshape(depth, T * BS, D_PLE)

    rows3 = lambda a: a.reshape(depth, 1, -1)
    cast = lambda a: a.astype(BF16)
    g_in3, gate_g3 = rows3(g_in), rows3(ple_gate_g)
    w_in_b, wo, wg, wp = cast(w_in), cast(w_out), cast(ple_w_gate), cast(ple_w_proj)
    wup, aup, g1, g2 = cast(rw_w_up), cast(rw_a_up), cast(s5_glu_w1), cast(s5_glu_w2)
    eye = jnp.eye(S5_GROUPS, dtype=F32)
    bd_in = lambda bb: jnp.einsum('lgpc,gh->lgchp', bb, eye).reshape(depth, D_S5, D_S5_STATE)
    bd_out = lambda c: jnp.einsum('lgcp,gh->lgphc', c, eye).reshape(depth, D_S5_STATE, D_S5)
    bblk = cast(jnp.concatenate([bd_in(bb_re), bd_in(bb_im)], axis=2))
    cblk = cast(jnp.concatenate([bd_out(s5_c_re), -bd_out(s5_c_im)], axis=1))
    hr_vecs = tuple(rows3(a) for a in (lb_all, hg_norm_g, rw_mu, rw_w0, rw_a0, rw_k_k, rw_k_a, rw_r_k,
                                       rw_gn_g, rw_gn_b))
    s5_vecs = (lam_re, lam_im, rows3(s5_d))

    carried = tuple(jnp.zeros(a.shape, F32) for a in (state_hgrn, state_rwkv, state_rwkv_shift))
    sre_s = state_s5_re.reshape(depth, BS, D_S5_STATE)
    sim_s = state_s5_im.reshape(depth, BS, D_S5_STATE)
    outs_p = [[] for _ in range(5)]
    for l in range(depth):
        final = l == depth - 1

        z = _dense_in(xp, g_in3, w_in_b, l)
        o, hg, rw, sh, sre, sim = _mix_prompt(
            z.reshape(B, L, D_IN), hr_vecs + s5_vecs, (wup, aup, bblk, cblk, g1, g2), consts, l)
        xp = _dense_out(xp, o.reshape(B * L, D_MIX), pp, wo, gate_g3, wg, wp, g_final2, l, final)
        for acc, val in zip(outs_p, (hg, rw, sh, sre.reshape(B, S5_GROUPS, S5_P), sim.reshape(B, S5_GROUPS, S5_P))):
            acc.append(val)

        zs = _dense_in(xs, g_in3, w_in_b, l)
        o_ab, *carried = _mix_sample(zs.reshape(T, BS, D_IN), state_hgrn, state_rwkv, state_rwkv_shift,
                                     tuple(carried), hr_vecs, wup, aup, ones_bd, l)
        o_c, sre_s, sim_s = _s5_sample(zs, sre_s, sim_s, s5_vecs + (bblk, cblk, g1, g2), T, BS, l)
        os_ = jnp.concatenate([o_ab.reshape(T * BS, 2 * D_HEADS), o_c], axis=1)
        xs = _dense_out(xs, os_, ps, wo, gate_g3, wg, wp, g_final2, l, final)

    y_prompt = xp.reshape(B, L, D_MODEL)
    y_sample = jnp.transpose(xs.reshape(T, BS, D_MODEL), (1, 0, 2))
    s5_shape = (depth, BS, S5_GROUPS, S5_P)
    return ((y_prompt, y_sample) + tuple(jnp.stack(a) for a in outs_p)
            + tuple(carried) + (sre_s.reshape(s5_shape), sim_s.reshape(s5_shape)))
```

```python
import functools
import math

import jax
import jax.numpy as jnp
import numpy as np
from jax import lax
from jax.experimental import pallas as pl
from jax.experimental.pallas import tpu as pltpu

F32 = jnp.float32
BF16 = jnp.bfloat16

D_MODEL = 1024
D_PLE = 256
HEADS = 6
HD = 64
D_HEADS = HEADS * HD
LORA = 64
D_SHIFT = 3 * D_HEADS + 2 * LORA
S5_GROUPS = 16
S5_CH = 16
S5_P = 64
D_S5 = S5_GROUPS * S5_CH
D_S5_STATE = S5_GROUPS * S5_P
D_IN = 4 * D_HEADS + D_SHIFT + D_HEADS + 2 * D_S5
D_MIX = 2 * D_HEADS + D_S5

C_Q, C_F, C_I, C_GA = 0, D_HEADS, 2 * D_HEADS, 3 * D_HEADS
C_RW = 4 * D_HEADS
C_GB = C_RW + D_SHIFT
C_U = C_GB + D_HEADS
C_GC = C_U + D_S5

LB_FLOOR = 1e-12
NORM_EPS = 1e-6
RW_GN_EPS = 64e-5
RW_DECAY_SCALE = math.exp(-0.5)

CHUNK = 64
HG_BASE = 4
ROW_TILE = 512
PITCH = CHUNK + 8
TPU_V7X_VMEM_LIMIT = 56 * 1024 * 1024


def _dot(a, b):
    return jnp.dot(a.astype(BF16), b.astype(BF16), preferred_element_type=F32)


def _dot_nt(a, b):
    return lax.dot_general(a.astype(BF16), b.astype(BF16), (((1,), (1,)), ((), ())), preferred_element_type=F32)


def _dot_tn(a, b):
    return lax.dot_general(a.astype(BF16), b.astype(BF16), (((0,), (0,)), ((), ())), preferred_element_type=F32)


def _split3(x):
    hi = x.astype(BF16)
    r1 = x - hi.astype(F32)
    mid = r1.astype(BF16)
    lo = (r1 - mid.astype(F32)).astype(BF16)
    return hi, mid, lo


def _sel_left(m, x):
    hi, mid, lo = _split3(x)
    d = lambda p: jnp.dot(m, p, preferred_element_type=F32)
    return d(hi) + d(mid) + d(lo)


def _head_sum(x, m):
    return jnp.dot(x.astype(BF16), m, preferred_element_type=F32)


def _sigmoid(x):
    return jax.nn.sigmoid(x)


def _silu(x):
    return x * jax.nn.sigmoid(x)


def _rmsnorm(x, g):
    return x * lax.rsqrt(jnp.mean(x * x, axis=-1, keepdims=True) + NORM_EPS) * g


@functools.lru_cache(maxsize=None)
def _chunk_consts():
    C = CHUNK
    t = np.arange(C)[:, None]
    i = np.arange(C)[None, :]
    tri = (i <= t).astype(np.float32)
    masks = []
    m = C // 2
    while m >= HG_BASE:
        same = (t // (2 * m)) == (i // (2 * m))
        masks.append((same & ((t % (2 * m)) >= m) & ((i % (2 * m)) < m)).astype(np.float32))
        m //= 2
    masks.append((((t // HG_BASE) == (i // HG_BASE)) & (i <= t)).astype(np.float32))
    hmask = np.stack(masks, 0)
    rmask = np.stack([(i < t).astype(np.float32), (i <= t).astype(np.float32)], 0)
    ones_bd = np.kron(np.eye(HEADS, dtype=np.float32), np.ones((HD, HD), np.float32))
    hmask = np.tile(hmask, (1, 1, 4))
    rmask = np.tile(rmask, (1, 1, 4))
    lane_unit = np.arange(4 * HD)[None, :] // HD
    gmask = np.stack([np.broadcast_to(lane_unit == g, (C, 4 * HD)) for g in range(4)], 0).astype(np.float32)
    bdmask = np.kron(np.eye(4, dtype=np.float32), np.ones((HD, HD), np.float32))
    return tri, hmask, rmask, ones_bd, gmask, bdmask


N_LEVELS = 4


def _dense_in_kernel(x_ref, g_ref, w_ref, z_ref):
    h = _rmsnorm(x_ref[...], g_ref[...])
    z_ref[...] = jnp.dot(h.astype(BF16), w_ref[...], preferred_element_type=F32)


def _layer(a, l):
    return pl.BlockSpec((None,) + a.shape[1:], lambda i, _n=a.ndim - 1: (l,) + (0,) * _n)


def _whole(a):
    return pl.BlockSpec(a.shape, lambda i, _n=a.ndim: (0,) * _n)


def _dense_in(x, g, w, l):
    rows = x.shape[0]
    tile = min(ROW_TILE, rows)
    return pl.pallas_call(
        _dense_in_kernel,
        out_shape=jax.ShapeDtypeStruct((rows, D_IN), F32),
        grid=(rows // tile,),
        in_specs=[pl.BlockSpec((tile, D_MODEL), lambda i: (i, 0)), _layer(g, l), _layer(w, l)],
        out_specs=pl.BlockSpec((tile, D_IN), lambda i: (i, 0)),
        compiler_params=pltpu.CompilerParams(
            dimension_semantics=("arbitrary",), vmem_limit_bytes=TPU_V7X_VMEM_LIMIT),
        name="dense_in",
    )(x, g, w)


def _dense_out_kernel(x_ref, o_ref, p_ref, wo_ref, gg_ref, wg_ref, wp_ref, gf_ref, y_ref, *, final):
    x1 = x_ref[...] + jnp.dot(o_ref[...], wo_ref[...], preferred_element_type=F32)
    gate = _sigmoid(jnp.dot(_rmsnorm(x1, gg_ref[...]).astype(BF16), wg_ref[...], preferred_element_type=F32))
    x2 = x1 + jnp.dot(p_ref[...].astype(BF16), wp_ref[...], preferred_element_type=F32) * gate
    if final:
        x2 = _rmsnorm(x2, gf_ref[...])
    y_ref[...] = x2


def _dense_out(x, o, p, wo, gg, wg, wp, gf, l, final):
    rows = x.shape[0]
    tile = min(ROW_TILE, rows)
    row_spec = lambda n: pl.BlockSpec((tile, n), lambda i: (i, 0))
    p_spec = pl.BlockSpec((None, tile, D_PLE), lambda i: (l, i, 0))
    return pl.pallas_call(
        functools.partial(_dense_out_kernel, final=final),
        out_shape=jax.ShapeDtypeStruct((rows, D_MODEL), F32),
        grid=(rows // tile,),
        in_specs=[row_spec(D_MODEL), row_spec(D_MIX), p_spec, _layer(wo, l), _layer(gg, l), _layer(wg, l),
                  _layer(wp, l), _whole(gf)],
        out_specs=row_spec(D_MODEL),
        compiler_params=pltpu.CompilerParams(
            dimension_semantics=("arbitrary",), vmem_limit_bytes=TPU_V7X_VMEM_LIMIT),
        name="dense_out",
    )(x, o, p, wo, gg, wg, wp, gf)


def _s5_scan(bu_ref, sall_ref, lre, lim, s_re, s_im, nt, nb):
    lr = jnp.broadcast_to(lre, (nb, D_S5_STATE))
    li = jnp.broadcast_to(lim, (nb, D_S5_STATE))

    def step(t, carry):
        sr, si = carry
        r0 = pl.multiple_of(t * nb, nb)
        br = bu_ref[pl.ds(r0, nb), 0:D_S5_STATE]
        bi = bu_ref[pl.ds(r0, nb), D_S5_STATE:2 * D_S5_STATE]
        nr = lr * sr - li * si + br
        ni = lr * si + li * sr + bi
        sall_ref[pl.ds(r0, nb), 0:D_S5_STATE] = nr
        sall_ref[pl.ds(r0, nb), D_S5_STATE:2 * D_S5_STATE] = ni
        return nr, ni

    return lax.fori_loop(0, nt, step, (s_re, s_im))


def _s5_head(u, sall_ref, cblk_ref, d_ref, g1_ref, g2_ref):
    y = jnp.dot(sall_ref[...].astype(BF16), cblk_ref[...], preferred_element_type=F32) + d_ref[...] * u
    y = jax.nn.gelu(y)
    yb = y.astype(BF16)
    return jnp.dot(yb, g1_ref[...], preferred_element_type=F32) * _sigmoid(
        jnp.dot(yb, g2_ref[...], preferred_element_type=F32))


PACK = 4
PW = PACK * HD
SEQ_PAIRS = 2
S5_STEPS = 16


def _interleave(required, background=()):
    live = list(required)
    extra = list(background)
    while live:
        for entry in list(live) + list(extra):
            gen, k = entry
            for _ in range(k):
                try:
                    next(gen)
                except StopIteration:
                    (live if entry in live else extra).remove(entry)
                    break


def _mix_prompt_packed_kernel(z_ref, lb_ref, hgn_ref, mu_ref, w0_ref, a0_ref, kk_ref, ka_ref, rk_ref, gng_ref,
                              gnb_ref, lre_ref, lim_ref, d_ref, wup_ref, aup_ref, bblk_ref, cblk_ref, g1_ref, g2_ref,
                              tri_ref, hmask_ref, rmask_ref, ones_ref, gmask_ref, bdmask_ref,
                              o_ref, hg_ref, rw_ref, sh_ref, sre_ref, sim_ref,
                              hg_bd0, hg_bd1, hg_bd2, rw_bd0, rw_bd1, rw_bd2, upad, up, bu, sall, ypad):
    nb = z_ref.shape[0]
    C = CHUNK
    n_pairs = D_HEADS // 128
    hg_bd = (hg_bd0, hg_bd1, hg_bd2)
    rw_bd = (rw_bd0, rw_bd1, rw_bd2)
    step = pl.program_id(0)

    @pl.when(step == 0)
    def _():
        for ref in hg_bd + rw_bd:
            ref[...] = jnp.zeros(ref.shape, F32)
        sh_ref[...] = jnp.zeros(sh_ref.shape, F32)
        sre_ref[...] = jnp.zeros(sre_ref.shape, F32)
        sim_ref[...] = jnp.zeros(sim_ref.shape, F32)

    ones_bd = ones_ref[...]
    bdmask = bdmask_ref[...]
    row_is0 = lax.broadcasted_iota(jnp.int32, (C, D_SHIFT), 0) == 0
    low_half = lax.broadcasted_iota(jnp.int32, (C, D_HEADS), 0) % 8 < HG_BASE

    def bd(x):
        m = x.shape[1] // PW
        blocks = []
        for g in range(PACK):
            mask = gmask_ref[g]
            if m > 1:
                mask = jnp.concatenate([mask] * m, axis=1)
            blocks.append(x * mask)
        return jnp.concatenate(blocks, axis=0)

    def prep_stages(b, p):
        q = z_ref[b, :, C_Q:C_Q + D_HEADS]
        f = z_ref[b, :, C_F:C_F + D_HEADS]
        lbs = jnp.maximum(lb_ref[...], LB_FLOOR)
        logf = jnp.log(lbs + (1.0 - lbs) * _sigmoid(f))
        kf = (1.0 - lbs) * _sigmoid(-f)
        qf = _silu(q)
        yield
        cum = _sel_left(tri_ref[...], logf)
        p["qin"] = (qf * jnp.exp(cum)).astype(BF16)
        p["kend"] = (kf * jnp.exp(cum[C - 1:C, :] - cum)).astype(BF16)
        yield
        for l in range(N_LEVELS):
            m = C >> (l + 1)
            ref = jnp.concatenate([jnp.broadcast_to(cum[q0 + m - 1:q0 + m, :], (2 * m, D_HEADS))
                                   for q0 in range(0, C, 2 * m)], axis=0)
            e = jnp.exp(-jnp.abs(cum - ref))
            p["qe%d" % l] = (qf * e).astype(BF16)
            p["ke%d" % l] = (kf * e).astype(BF16)
            yield
        first = jnp.concatenate([jnp.broadcast_to(cum[q0:q0 + 1, :], (8, D_HEADS)) for q0 in range(0, C, 8)], axis=0)
        second = jnp.concatenate([jnp.broadcast_to(cum[q0 + HG_BASE:q0 + HG_BASE + 1, :], (8, D_HEADS))
                                  for q0 in range(0, C, 8)], axis=0)
        dq = cum - jnp.where(low_half, first, second)
        p["qe%d" % N_LEVELS] = (qf * jnp.exp(dq)).astype(BF16)
        p["ke%d" % N_LEVELS] = (kf * jnp.exp(-dq)).astype(BF16)
        p["gcol"] = jnp.exp(cum[C - 8:C, :].T)[:, 7:8]
        p["ib"] = z_ref[b, :, C_I:C_I + D_HEADS].astype(BF16)
        yield

        c = z_ref[b, :, C_RW:C_RW + D_SHIFT]
        prev = jnp.where(row_is0, sh_ref[pl.ds(b, 1), :], pltpu.roll(c, 1, axis=0))
        p["last"] = c[C - 1:C, :]
        cs = c + mu_ref[...] * (prev - c)
        r = cs[:, 0:D_HEADS]
        k = cs[:, D_HEADS:2 * D_HEADS]
        v = cs[:, 2 * D_HEADS:3 * D_HEADS]
        wd = cs[:, 3 * D_HEADS:3 * D_HEADS + LORA]
        ad = cs[:, 3 * D_HEADS + LORA:D_SHIFT]
        logw = -RW_DECAY_SCALE * _sigmoid(w0_ref[...] + _dot(jnp.tanh(wd), wup_ref[...]))
        a = _sigmoid(a0_ref[...] + _dot(ad, aup_ref[...]))
        yield
        kk = k * kk_ref[...]
        kk = kk * lax.rsqrt(jnp.maximum(_head_sum(kk * kk, ones_bd), 1e-24))
        kt = k * (1.0 + (a - 1.0) * ka_ref[...])
        beta = kk * a
        p["bonus"] = _head_sum(r * kt * rk_ref[...], ones_bd)
        cl = _sel_left(tri_ref[...], logw)
        yield
        p["aq"] = kk * jnp.exp(cl - logw)
        p["rq"] = r * jnp.exp(cl)
        ecl = jnp.exp(-cl)
        p["kd"] = (kt * ecl).astype(BF16)
        p["bd"] = (beta * ecl).astype(BF16)
        eend = jnp.exp(cl[C - 1:C, :] - cl)
        p["ktl"] = kt * eend
        p["btl"] = beta * eend
        p["gam"] = jnp.exp(cl[C - 1:C, :])
        p["v"] = v

    def chain_stages(i, ps):
        b0 = 2 * SEQ_PAIRS * i
        pairs = range(SEQ_PAIRS * n_pairs)
        pj = [u % n_pairs for u in pairs]
        pq = [2 * (u // n_pairs) for u in pairs]
        tiles = [slice(pj[u] * 128, (pj[u] + 1) * 128) for u in pairs]
        slot = [SEQ_PAIRS * i + u // n_pairs for u in pairs]

        def x4(name, rows=False):
            if rows:
                return [jnp.concatenate([ps[pq[u]][name][tiles[u], :], ps[pq[u] + 1][name][tiles[u], :]], axis=0)
                        for u in pairs]
            return [jnp.concatenate([ps[pq[u]][name][:, tiles[u]], ps[pq[u] + 1][name][:, tiles[u]]], axis=1)
                    for u in pairs]

        s_hg = [hg_bd[pj[u]][slot[u]] for u in pairs]
        s_rw = [rw_bd[pj[u]][slot[u]] for u in pairs]
        aq4, rq4, v4, btl4, ktl4 = x4("aq"), x4("rq"), x4("v"), x4("btl"), x4("ktl")
        bd4, kd4 = x4("bd"), x4("kd")
        ar = [jnp.concatenate([aq4[j], rq4[j]], axis=0).astype(BF16) for j in pairs]
        zb = [_dot_nt(ar[j], bd(bd4[j])) for j in pairs]
        zk = [_dot_nt(ar[j], bd(kd4[j])) for j in pairs]
        yield
        mb = [zb[j][0:C] * rmask_ref[0] for j in pairs]
        nbm = [zb[j][C:2 * C] * rmask_ref[1] for j in pairs]
        mk = [zk[j][0:C] * rmask_ref[0] for j in pairs]
        nk = [zk[j][C:2 * C] * rmask_ref[1] for j in pairs]
        bdv = [bd(v4[j].astype(BF16)) for j in pairs]
        x0 = [jnp.concatenate([aq4[j], _dot(mk[j], bdv[j])], axis=1) for j in pairs]
        yield
        bm = [-m for m in mb]
        pw = [_dot(mb[j], bd(mb[j].astype(BF16))) for j in pairs]
        att = [None for _ in pairs]
        level = 0
        n = 2
        while 2 * n < C:
            yield
            both = [_dot(jnp.concatenate([bm[j], pw[j]], axis=0), bd(pw[j].astype(BF16))) for j in pairs]
            bm = [bm[j] + pw[j] + both[j][0:C] for j in pairs]
            pw = [both[j][C:2 * C] for j in pairs]
            n *= 2
            qe4, ke4 = x4("qe%d" % level), x4("ke%d" % level)
            for j in pairs:
                term = hmask_ref[level] * _dot_nt(qe4[j], bd(ke4[j]))
                att[j] = term if att[j] is None else att[j] + term
            level += 1
        yield
        bm = [bm[j] + pw[j] + _dot(bm[j], bd(pw[j].astype(BF16))) for j in pairs]
        while level <= N_LEVELS:
            qe4, ke4 = x4("qe%d" % level), x4("ke%d" % level)
            for j in pairs:
                term = hmask_ref[level] * _dot_nt(qe4[j], bd(ke4[j]))
                att[j] = term if att[j] is None else att[j] + term
            level += 1
        yield
        ib4, qin4, kend4 = x4("ib"), x4("qin"), x4("kend")
        o4 = [_dot(att[j], bd(ib4[j])) + _dot(qin4[j], s_hg[j]) for j in pairs]
        gcol4 = x4("gcol", rows=True)
        new_hg = [gcol4[j] * s_hg[j] + bdmask * _dot_tn(kend4[j], ib4[j]) for j in pairs]
        x = [x0[j] + _dot(bm[j], bd(x0[j].astype(BF16))) for j in pairs]
        yield
        bdx = [bd(x[j].astype(BF16)) for j in pairs]
        w = [bdmask * _dot_tn(btl4[j], x[j][:, 0:PW]) for j in pairs]
        gt = [bdmask * _dot_tn(jnp.concatenate([v4[j], x[j][:, PW:2 * PW]], axis=0),
                               jnp.concatenate([ktl4[j], -btl4[j]], axis=0)) for j in pairs]
        nx = [_dot(nbm[j], bdx[j]) for j in pairs]
        yield
        y4 = [_dot_nt(rq4[j] - nx[j][:, 0:PW], s_rw[j]) + _dot(nk[j], bdv[j]) - nx[j][:, PW:2 * PW] for j in pairs]
        gam4 = x4("gam")
        new_rw = [s_rw[j] * gam4[j] - _dot_nt(s_rw[j], w[j]) + gt[j] for j in pairs]
        for j in pairs:
            hg_bd[pj[j]][slot[j]] = new_hg[j]
            rw_bd[pj[j]][slot[j]] = new_rw[j]
        yield
        o_tiles = [[o4[(u // 2) * n_pairs + j][:, (u % 2) * 128:(u % 2 + 1) * 128] for j in range(n_pairs)]
                   for u in range(2 * SEQ_PAIRS)]
        y_tiles = [[y4[(u // 2) * n_pairs + j][:, (u % 2) * 128:(u % 2 + 1) * 128] for j in range(n_pairs)]
                   for u in range(2 * SEQ_PAIRS)]
        for u in range(2 * SEQ_PAIRS):
            b = b0 + u
            o = jnp.concatenate(o_tiles[u], axis=1)
            ssq = _head_sum(o * o, ones_bd)
            o = o * lax.rsqrt(ssq * (1.0 / HD) + NORM_EPS) * hgn_ref[...]
            o_ref[b, :, 0:D_HEADS] = (o * _silu(z_ref[b, :, C_GA:C_GA + D_HEADS])).astype(BF16)
            y = jnp.concatenate(y_tiles[u], axis=1)
            yc = y - _head_sum(y, ones_bd) * (1.0 / HD)
            var = _head_sum(yc * yc, ones_bd) * (1.0 / HD)
            yn = yc * lax.rsqrt(var + RW_GN_EPS) * gng_ref[...] + gnb_ref[...] + ps[u]["bonus"] * ps[u]["v"]
            o_ref[b, :, D_HEADS:2 * D_HEADS] = (yn * _silu(z_ref[b, :, C_GB:C_GB + D_HEADS])).astype(BF16)
            sh_ref[pl.ds(b, 1), :] = ps[u]["last"]
            yield

    n_lane = D_S5 // 128
    n_groups = nb // (2 * SEQ_PAIRS)
    s5_blocks = C // S5_STEPS // n_groups
    blk_rows = S5_STEPS * nb

    def s5_stages(i, state):
        lr = jnp.broadcast_to(lre_ref[...], (nb, D_S5_STATE))
        li = jnp.broadcast_to(lim_ref[...], (nb, D_S5_STATE))
        sr, si = state
        for k in range(s5_blocks):
            t0 = (i * s5_blocks + k) * S5_STEPS
            row0 = pl.multiple_of(t0 * nb, blk_rows)
            for tt in range(S5_STEPS):
                for j in range(n_lane):
                    up[pl.ds(pl.multiple_of(row0 + tt * nb, nb), nb), j * 128:(j + 1) * 128] = (
                        upad[j, pl.ds(t0 + tt, nb, stride=PITCH), :])
            u = up[pl.ds(row0, blk_rows), :]
            bu[pl.ds(row0, blk_rows), :] = jnp.dot(u.astype(BF16), bblk_ref[...], preferred_element_type=F32)
            yield
            for tt in range(S5_STEPS):
                r0 = pl.multiple_of(row0 + tt * nb, nb)
                br = bu[pl.ds(r0, nb), 0:D_S5_STATE]
                bi = bu[pl.ds(r0, nb), D_S5_STATE:2 * D_S5_STATE]
                sr, si = lr * sr - li * si + br, lr * si + li * sr + bi
                sall[pl.ds(r0, nb), 0:D_S5_STATE] = sr
                sall[pl.ds(r0, nb), D_S5_STATE:2 * D_S5_STATE] = si
                if tt % 4 == 3:
                    yield
            y = jnp.dot(sall[pl.ds(row0, blk_rows), :].astype(BF16), cblk_ref[...],
                        preferred_element_type=F32) + d_ref[...] * u
            yb = jax.nn.gelu(y).astype(BF16)
            yg = jnp.dot(yb, g1_ref[...], preferred_element_type=F32) * _sigmoid(
                jnp.dot(yb, g2_ref[...], preferred_element_type=F32))
            for tt in range(S5_STEPS):
                for j in range(n_lane):
                    ypad[j, pl.ds(t0 + tt, nb, stride=PITCH), :] = yg[tt * nb:(tt + 1) * nb, j * 128:(j + 1) * 128]
            yield
        state[0], state[1] = sr, si

    def prep_group(i, ps):
        for q in range(2 * SEQ_PAIRS):
            yield from prep_stages(2 * SEQ_PAIRS * i + q, ps[q])

    for b in range(nb):
        for j in range(n_lane):
            upad[j, b * PITCH:b * PITCH + C, :] = z_ref[b, :, C_U + j * 128:C_U + (j + 1) * 128]

    def group_body(i, carry):
        ps = [{} for _ in range(2 * SEQ_PAIRS)]
        state = list(carry)
        s5 = (s5_stages(i, state), 1)
        _interleave([(prep_group(i, ps), 1)])
        _interleave([(chain_stages(i, ps), 1)], background=[s5])
        _interleave([s5])
        return tuple(state)

    s_re, s_im = lax.fori_loop(0, n_groups, group_body, (sre_ref[...], sim_ref[...]))
    sre_ref[...] = s_re
    sim_ref[...] = s_im
    for b in range(nb):
        for j in range(n_lane):
            gate = _silu(z_ref[b, :, C_GC + j * 128:C_GC + (j + 1) * 128])
            o_ref[b, :, 2 * D_HEADS + j * 128:2 * D_HEADS + (j + 1) * 128] = (
                ypad[j, b * PITCH:b * PITCH + C, :] * gate).astype(BF16)

    @pl.when(step == pl.num_programs(0) - 1)
    def _():
        for i in range(nb // 2):
            for j in range(n_pairs):
                for g in range(PACK):
                    b = 2 * i + g // 2
                    h = 2 * j + g % 2
                    blk = slice(g * HD, (g + 1) * HD)
                    hg_ref[b, h] = hg_bd[j][i, blk, :][:, blk]
                    rw_ref[b, h] = rw_bd[j][i, blk, :][:, blk]


def _mix_prompt(z, vecs, mats, consts, l):
    B, L, _ = z.shape
    nt = L // CHUNK
    out_shape = (
        jax.ShapeDtypeStruct((B, L, D_MIX), BF16),
        jax.ShapeDtypeStruct((B, HEADS, HD, HD), F32),
        jax.ShapeDtypeStruct((B, HEADS, HD, HD), F32),
        jax.ShapeDtypeStruct((B, D_SHIFT), F32),
        jax.ShapeDtypeStruct((B, D_S5_STATE), F32),
        jax.ShapeDtypeStruct((B, D_S5_STATE), F32),
    )
    out_specs = (
        pl.BlockSpec((B, CHUNK, D_MIX), lambda i: (0, i, 0)),
        pl.BlockSpec((B, HEADS, HD, HD), lambda i: (0, 0, 0, 0)),
        pl.BlockSpec((B, HEADS, HD, HD), lambda i: (0, 0, 0, 0)),
        pl.BlockSpec((B, D_SHIFT), lambda i: (0, 0)),
        pl.BlockSpec((B, D_S5_STATE), lambda i: (0, 0)),
        pl.BlockSpec((B, D_S5_STATE), lambda i: (0, 0)),
    )
    args = (z,) + tuple(vecs) + tuple(mats) + tuple(consts)
    in_specs = ([pl.BlockSpec((B, CHUNK, D_IN), lambda i: (0, i, 0))]
                + [_layer(a, l) for a in tuple(vecs) + tuple(mats)] + [_whole(a) for a in consts])
    rows = B * CHUNK
    assert B % (2 * SEQ_PAIRS) == 0 and D_HEADS // 128 == 3
    scratch = [pltpu.VMEM((B // 2, PW, PW), F32) for _ in range(6)] + [
        pltpu.VMEM((D_S5 // 128, B * PITCH, 128), F32),
        pltpu.VMEM((rows, D_S5), F32),
        pltpu.VMEM((rows, 2 * D_S5_STATE), F32),
        pltpu.VMEM((rows, 2 * D_S5_STATE), F32),
        pltpu.VMEM((D_S5 // 128, B * PITCH, 128), F32),
    ]
    return pl.pallas_call(
        _mix_prompt_packed_kernel,
        out_shape=out_shape,
        grid=(nt,),
        in_specs=in_specs,
        out_specs=out_specs,
        scratch_shapes=scratch,
        compiler_params=pltpu.CompilerParams(
            dimension_semantics=("arbitrary",), vmem_limit_bytes=TPU_V7X_VMEM_LIMIT),
        name="mix_prompt",
    )(*args)


def _mix_sample_lanes_kernel(zq, zf, zi, zga, zr, zk, zv, zwd, zad, zgb, pr, pk, pv, pwd, pad_,
                             mur, muk, muv, muwd, muad, lb_ref, hgn_ref, w0_ref, a0_ref, kk_ref, ka_ref, rk_ref,
                             gng_ref, gnb_ref, wupT_ref, aupT_ref, hg_in, rw_in, hg_prev, rw_prev,
                             oa_ref, ob_ref, hg_ref, rw_ref,
                             fg_s, kf_s, qf_s, iv_s, kk_s, dec_s, beta_s, kt_s, r_s, v_s, y_s):
    del hg_prev, rw_prev
    nb = pr.shape[1]
    T = zq.shape[1] // nb
    lbs = jnp.maximum(lb_ref[...], LB_FLOOR)
    col = lambda ref, t: ref[:, t * nb:(t + 1) * nb]

    def shifted(z, p, mu, t):
        c = col(z, t)
        prev = p[...] if t == 0 else col(z, t - 1)
        return c + mu[...] * (prev - c)

    for t in range(T):
        f = col(zf, t)
        fg_s[t] = lbs + (1.0 - lbs) * _sigmoid(f)
        kf_s[t] = (1.0 - lbs) * _sigmoid(-f)
        qf_s[t] = _silu(col(zq, t))
        iv_s[t] = col(zi, t)
        r = shifted(zr, pr, mur, t)
        k = shifted(zk, pk, muk, t)
        v = shifted(zv, pv, muv, t)
        wd = shifted(zwd, pwd, muwd, t)
        ad = shifted(zad, pad_, muad, t)
        logw = -RW_DECAY_SCALE * _sigmoid(w0_ref[...] + _dot(wupT_ref[...], jnp.tanh(wd)))
        a = _sigmoid(a0_ref[...] + _dot(aupT_ref[...], ad))
        kk = k * kk_ref[...]
        kk = kk * lax.rsqrt(jnp.maximum(jnp.sum(kk * kk, axis=0, keepdims=True), 1e-24))
        kk_s[t] = kk
        dec_s[t] = jnp.exp(logw)
        beta_s[t] = kk * a
        kt_s[t] = k * (1.0 + (a - 1.0) * ka_ref[...])
        r_s[t] = r
        v_s[t] = v

    def hg_row(k, acc):
        s = hg_in[k]
        acc = list(acc)
        for t in range(T):
            row = lambda ref: ref[t, pl.ds(k, 1), :]
            s = row(fg_s) * s + row(kf_s) * iv_s[t]
            acc[t] = acc[t] + row(qf_s) * s
        hg_ref[k] = s
        return tuple(acc)

    zero = jnp.zeros((HD, nb), F32)
    o = lax.fori_loop(0, HD, hg_row, (zero,) * T)
    for t in range(T):
        ssq = jnp.sum(o[t] * o[t], axis=0, keepdims=True)
        on = o[t] * lax.rsqrt(ssq * (1.0 / HD) + NORM_EPS) * hgn_ref[...]
        oa_ref[:, t * nb:(t + 1) * nb] = (on * _silu(col(zga, t))).astype(BF16)

    def rw_row(vi, carry):
        s = rw_in[vi]
        for t in range(T):
            sa = jnp.sum(s * kk_s[t], axis=0, keepdims=True)
            s = s * dec_s[t] - sa * beta_s[t] + v_s[t, pl.ds(vi, 1), :] * kt_s[t]
            y_s[t, pl.ds(vi, 1), :] = jnp.sum(s * r_s[t], axis=0, keepdims=True)
        rw_ref[vi] = s
        return carry

    lax.fori_loop(0, HD, rw_row, 0, unroll=4)
    for t in range(T):
        y = y_s[t]
        yc = y - jnp.mean(y, axis=0, keepdims=True)
        var = jnp.mean(yc * yc, axis=0, keepdims=True)
        bonus = jnp.sum(r_s[t] * kt_s[t] * rk_ref[...], axis=0, keepdims=True)
        yn = yc * lax.rsqrt(var + RW_GN_EPS) * gng_ref[...] + gnb_ref[...] + bonus * v_s[t]
        ob_ref[:, t * nb:(t + 1) * nb] = (yn * _silu(col(zgb, t))).astype(BF16)


def _mix_sample_lanes(zT, shT, hgT, rwT, carried, cols, mu_cols, wupT, aupT, l):
    B = shT.shape[2]
    TB = zT.shape[1]
    zblk = lambda off: pl.BlockSpec((HD, TB), lambda h: (off // HD + h, 0))
    zone = lambda off: pl.BlockSpec((HD, TB), lambda h: (off // HD, 0))
    sblk = lambda off: pl.BlockSpec((None, HD, B), lambda h: (l, off // HD + h, 0))
    sone = lambda off: pl.BlockSpec((None, HD, B), lambda h: (l, off // HD, 0))
    r0, k0, v0 = 0, D_HEADS, 2 * D_HEADS
    wd0, ad0 = 3 * D_HEADS, 3 * D_HEADS + LORA
    state = pl.BlockSpec((None, None, HD, HD, B), lambda h: (l, h, 0, 0, 0))
    head_cols = pl.BlockSpec((None, HD, B), lambda h: (l, h, 0))
    lora = pl.BlockSpec((None, HD, LORA), lambda h: (l, h, 0))
    in_specs = ([zblk(C_Q), zblk(C_F), zblk(C_I), zblk(C_GA), zblk(C_RW + r0), zblk(C_RW + k0), zblk(C_RW + v0),
                 zone(C_RW + wd0), zone(C_RW + ad0), zblk(C_GB)]
                + [sblk(r0), sblk(k0), sblk(v0), sone(wd0), sone(ad0)]
                + [sblk(r0), sblk(k0), sblk(v0), sone(wd0), sone(ad0)]
                + [head_cols] * len(cols) + [lora, lora, state, state]
                + [pl.BlockSpec(memory_space=pl.ANY)] * 2)
    out_row = pl.BlockSpec((HD, TB), lambda h: (h, 0))
    slab = pltpu.VMEM((TB // B, HD, B), F32)
    n_in = len(in_specs)
    return pl.pallas_call(
        _mix_sample_lanes_kernel,
        out_shape=(
            jax.ShapeDtypeStruct((D_HEADS, TB), BF16),
            jax.ShapeDtypeStruct((D_HEADS, TB), BF16),
            jax.ShapeDtypeStruct(hgT.shape, F32),
            jax.ShapeDtypeStruct(rwT.shape, F32),
        ),
        grid=(HEADS,),
        in_specs=in_specs,
        out_specs=(out_row, out_row, state, state),
        input_output_aliases={n_in - 2: 2, n_in - 1: 3},
        scratch_shapes=[slab] * 11,
        compiler_params=pltpu.CompilerParams(
            dimension_semantics=("arbitrary",), vmem_limit_bytes=TPU_V7X_VMEM_LIMIT),
        name="mix_sample",
    )(*([zT] * 10 + [shT] * 5 + [mu_cols] * 5 + list(cols) + [wupT, aupT, hgT, rwT] + list(carried)))


def _s5_sample_kernel(z_ref, sre_in, sim_in, lre_ref, lim_ref, d_ref, bblk_ref, cblk_ref, g1_ref, g2_ref,
                      o_ref, sre_ref, sim_ref, bu, sall, *, nt, nb):
    u = z_ref[:, C_U:C_U + D_S5]
    bu[...] = jnp.dot(u.astype(BF16), bblk_ref[...], preferred_element_type=F32)
    s_re, s_im = _s5_scan(bu, sall, lre_ref[...], lim_ref[...], sre_in[...], sim_in[...], nt, nb)
    sre_ref[...] = s_re
    sim_ref[...] = s_im
    yg = _s5_head(u, sall, cblk_ref, d_ref, g1_ref, g2_ref)
    o_ref[...] = (yg * _silu(z_ref[:, C_GC:C_GC + D_S5])).astype(BF16)


def _s5_sample(z2d, sre, sim, stacked, nt, nb, l):
    rows = z2d.shape[0]
    return pl.pallas_call(
        functools.partial(_s5_sample_kernel, nt=nt, nb=nb),
        out_shape=(
            jax.ShapeDtypeStruct((rows, D_S5), BF16),
            jax.ShapeDtypeStruct(sre.shape, F32),
            jax.ShapeDtypeStruct(sim.shape, F32),
        ),
        grid=(1,),
        in_specs=[_whole(z2d), _layer(sre, l), _layer(sim, l)] + [_layer(a, l) for a in stacked],
        out_specs=(pl.BlockSpec((rows, D_S5), lambda i: (0, 0)), _layer(sre, l), _layer(sim, l)),
        input_output_aliases={1: 1, 2: 2},
        scratch_shapes=[pltpu.VMEM((rows, 2 * D_S5_STATE), F32), pltpu.VMEM((rows, 2 * D_S5_STATE), F32)],
        compiler_params=pltpu.CompilerParams(
            dimension_semantics=("arbitrary",), vmem_limit_bytes=TPU_V7X_VMEM_LIMIT),
        name="s5_sample",
    )(z2d, sre, sim, *stacked)


def _prep_kernel(lbraw_ref, are_ref, aim_ref, ldt_ref, bre_ref, bim_ref, lb_ref, lre_ref, lim_ref, bbre_ref, bbim_ref):
    raw = lbraw_ref[...]
    e = jnp.exp(raw - jnp.max(raw, axis=0, keepdims=True))
    sm = e / jnp.sum(e, axis=0, keepdims=True)
    depth = raw.shape[0]
    acc = jnp.zeros_like(sm[0:1])
    for l in range(depth):
        acc = acc + sm[l:l + 1]
        lb_ref[l:l + 1, :] = acc - sm[0:1]
    a_re = are_ref[...]
    a_im = aim_ref[...]
    dt = jnp.exp(ldt_ref[...])
    mag = jnp.exp(dt * a_re)
    lam_re = mag * jnp.cos(dt * a_im)
    lam_im = mag * jnp.sin(dt * a_im)
    den = a_re * a_re + a_im * a_im
    xr = lam_re - 1.0
    f_re = (xr * a_re + lam_im * a_im) / den
    f_im = (lam_im * a_re - xr * a_im) / den
    lre_ref[...] = lam_re
    lim_ref[...] = lam_im
    b_re = bre_ref[...]
    b_im = bim_ref[...]
    bbre_ref[...] = f_re * b_re - f_im * b_im
    bbim_ref[...] = f_re * b_im + f_im * b_re


def _prep(hg_lb_raw, s5_a_re, s5_a_im, s5_log_dt, s5_b_re, s5_b_im):
    depth = hg_lb_raw.shape[0]
    n = depth * S5_GROUPS * S5_P
    col = lambda a: a.reshape(n, 1)
    ldt = jnp.broadcast_to(s5_log_dt[:, :, None], (depth, S5_GROUPS, S5_P))
    lb, lre, lim, bbre, bbim = pl.pallas_call(
        _prep_kernel,
        out_shape=(
            jax.ShapeDtypeStruct((depth, D_HEADS), F32),
            jax.ShapeDtypeStruct((n, 1), F32),
            jax.ShapeDtypeStruct((n, 1), F32),
            jax.ShapeDtypeStruct((n, S5_CH), F32),
            jax.ShapeDtypeStruct((n, S5_CH), F32),
        ),
        name="param_prep",
    )(hg_lb_raw, col(s5_a_re), col(s5_a_im), col(ldt), s5_b_re.reshape(n, S5_CH), s5_b_im.reshape(n, S5_CH))
    shape3 = (depth, S5_GROUPS, S5_P)
    return (lb, lre.reshape(depth, 1, D_S5_STATE), lim.reshape(depth, 1, D_S5_STATE),
            bbre.reshape(shape3 + (S5_CH,)), bbim.reshape(shape3 + (S5_CH,)))


def kernel(x_prompt, x_sample, state_hgrn, state_rwkv, state_rwkv_shift, state_s5_re, state_s5_im, p_prompt, p_sample, g_in, w_in, hg_lb_raw, hg_norm_g, rw_mu, rw_w0, rw_w_up, rw_a0, rw_a_up, rw_k_k, rw_k_a, rw_r_k, rw_gn_g, rw_gn_b, s5_a_re, s5_a_im, s5_log_dt, s5_b_re, s5_b_im, s5_c_re, s5_c_im, s5_d, s5_glu_w1, s5_glu_w2, w_out, ple_w_proj, ple_gate_g, ple_w_gate, g_final):
    depth = w_in.shape[0]
    B, L, _ = x_prompt.shape
    BS, T, _ = x_sample.shape
    tri, hmask, rmask, ones_np, gmask, bdmask = _chunk_consts()
    ones_bd = jnp.asarray(ones_np, BF16)
    consts = (jnp.asarray(tri, BF16), jnp.asarray(hmask, F32), jnp.asarray(rmask, F32), ones_bd,
              jnp.asarray(gmask, BF16), jnp.asarray(bdmask, F32))

    lb_all, lam_re, lam_im, bb_re, bb_im = _prep(hg_lb_raw, s5_a_re, s5_a_im, s5_log_dt, s5_b_re, s5_b_im)
    g_final2 = g_final.reshape(1, -1)

    xp = x_prompt.reshape(B * L, D_MODEL)
    xs = jnp.transpose(x_sample, (1, 0, 2)).reshape(T * BS, D_MODEL)
    pp = p_prompt.reshape(depth, B * L, D_PLE)
    ps = jnp.transpose(p_sample, (0, 2, 1, 3)).reshape(depth, T * BS, D_PLE)

    rows3 = lambda a: a.reshape(depth, 1, -1)
    cast = lambda a: a.astype(BF16)
    g_in3, gate_g3 = rows3(g_in), rows3(ple_gate_g)
    w_in_b, wo, wg, wp = cast(w_in), cast(w_out), cast(ple_w_gate), cast(ple_w_proj)
    wup, aup, g1, g2 = cast(rw_w_up), cast(rw_a_up), cast(s5_glu_w1), cast(s5_glu_w2)
    eye = jnp.eye(S5_GROUPS, dtype=F32)
    bd_in = lambda bb: jnp.einsum('lgpc,gh->lgchp', bb, eye).reshape(depth, D_S5, D_S5_STATE)
    bd_out = lambda c: jnp.einsum('lgcp,gh->lgphc', c, eye).reshape(depth, D_S5_STATE, D_S5)
    bblk = cast(jnp.concatenate([bd_in(bb_re), bd_in(bb_im)], axis=2))
    cblk = cast(jnp.concatenate([bd_out(s5_c_re), -bd_out(s5_c_im)], axis=1))
    hr_vecs = tuple(rows3(a) for a in (lb_all, hg_norm_g, rw_mu, rw_w0, rw_a0, rw_k_k, rw_k_a, rw_r_k,
                                       rw_gn_g, rw_gn_b))
    s5_vecs = (lam_re, lam_im, rows3(s5_d))

    hg_t = jnp.transpose(state_hgrn, (0, 2, 3, 4, 1))
    rw_t = jnp.transpose(state_rwkv, (0, 2, 3, 4, 1))
    sh_t = jnp.transpose(state_rwkv_shift, (0, 2, 1))
    lanes = lambda a: jnp.broadcast_to(a[:, :, None], a.shape + (BS,))
    s_cols = tuple(lanes(a) for a in (lb_all, hg_norm_g, rw_w0, rw_a0, rw_k_k, rw_k_a, rw_r_k, rw_gn_g, rw_gn_b))
    mu_cols = lanes(rw_mu)
    wup_t = cast(jnp.transpose(rw_w_up, (0, 2, 1)))
    aup_t = cast(jnp.transpose(rw_a_up, (0, 2, 1)))
    carried = (jnp.zeros(hg_t.shape, F32), jnp.zeros(rw_t.shape, F32))
    sh_out = []
    sre_s = state_s5_re.reshape(depth, BS, D_S5_STATE)
    sim_s = state_s5_im.reshape(depth, BS, D_S5_STATE)
    outs_p = [[] for _ in range(5)]
    for l in range(depth):
        final = l == depth - 1

        z = _dense_in(xp, g_in3, w_in_b, l)
        o, hg, rw, sh, sre, sim = _mix_prompt(
            z.reshape(B, L, D_IN), hr_vecs + s5_vecs, (wup, aup, bblk, cblk, g1, g2), consts, l)
        xp = _dense_out(xp, o.reshape(B * L, D_MIX), pp, wo, gate_g3, wg, wp, g_final2, l, final)
        for acc, val in zip(outs_p, (hg, rw, sh, sre.reshape(B, S5_GROUPS, S5_P), sim.reshape(B, S5_GROUPS, S5_P))):
            acc.append(val)

        zs = _dense_in(xs, g_in3, w_in_b, l)
        oa_t, ob_t, *carried = _mix_sample_lanes(zs.T, sh_t, hg_t, rw_t, tuple(carried), s_cols, mu_cols,
                                                 wup_t, aup_t, l)
        sh_out.append(zs[(T - 1) * BS:, C_RW:C_RW + D_SHIFT])
        o_c, sre_s, sim_s = _s5_sample(zs, sre_s, sim_s, s5_vecs + (bblk, cblk, g1, g2), T, BS, l)
        os_ = jnp.concatenate([oa_t.T, ob_t.T, o_c], axis=1)
        xs = _dense_out(xs, os_, ps, wo, gate_g3, wg, wp, g_final2, l, final)

    y_prompt = xp.reshape(B, L, D_MODEL)
    y_sample = jnp.transpose(xs.reshape(T, BS, D_MODEL), (1, 0, 2))
    s5_shape = (depth, BS, S5_GROUPS, S5_P)
    hg_s, rw_s = (jnp.transpose(a, (0, 4, 1, 2, 3)) for a in carried)
    return ((y_prompt, y_sample) + tuple(jnp.stack(a) for a in outs_p)
            + (hg_s, rw_s, jnp.stack(sh_out), sre_s.reshape(s5_shape), sim_s.reshape(s5_shape)))
```

```python
import functools
import math

import jax
import jax.numpy as jnp
import numpy as np
from jax import lax
from jax.experimental import pallas as pl
from jax.experimental.pallas import tpu as pltpu

F32 = jnp.float32
BF16 = jnp.bfloat16

D_MODEL = 1024
D_PLE = 256
HEADS = 6
HD = 64
D_HEADS = HEADS * HD
LORA = 64
D_SHIFT = 3 * D_HEADS + 2 * LORA
S5_GROUPS = 16
S5_CH = 16
S5_P = 64
D_S5 = S5_GROUPS * S5_CH
D_S5_STATE = S5_GROUPS * S5_P
D_IN = 4 * D_HEADS + D_SHIFT + D_HEADS + 2 * D_S5
D_MIX = 2 * D_HEADS + D_S5

C_Q, C_F, C_I, C_GA = 0, D_HEADS, 2 * D_HEADS, 3 * D_HEADS
C_RW = 4 * D_HEADS
C_GB = C_RW + D_SHIFT
C_U = C_GB + D_HEADS
C_GC = C_U + D_S5

LB_FLOOR = 1e-12
NORM_EPS = 1e-6
RW_GN_EPS = 64e-5
RW_DECAY_SCALE = math.exp(-0.5)

CHUNK = 64
HG_BASE = 4
ROW_TILE = 512
PITCH = CHUNK + 8
TPU_V7X_VMEM_LIMIT = 56 * 1024 * 1024


def _dot(a, b):
    return jnp.dot(a.astype(BF16), b.astype(BF16), preferred_element_type=F32)


def _dot_nt(a, b):
    return lax.dot_general(a.astype(BF16), b.astype(BF16), (((1,), (1,)), ((), ())), preferred_element_type=F32)


def _dot_tn(a, b):
    return lax.dot_general(a.astype(BF16), b.astype(BF16), (((0,), (0,)), ((), ())), preferred_element_type=F32)


def _split3(x):
    hi = x.astype(BF16)
    r1 = x - hi.astype(F32)
    mid = r1.astype(BF16)
    lo = (r1 - mid.astype(F32)).astype(BF16)
    return hi, mid, lo


def _sel_left(m, x):
    hi, mid, lo = _split3(x)
    d = lambda p: jnp.dot(m, p, preferred_element_type=F32)
    return d(hi) + d(mid) + d(lo)


def _head_sum(x, m):
    return jnp.dot(x.astype(BF16), m, preferred_element_type=F32)


def _sigmoid(x):
    return jax.nn.sigmoid(x)


def _silu(x):
    return x * jax.nn.sigmoid(x)


def _rmsnorm(x, g):
    return x * lax.rsqrt(jnp.mean(x * x, axis=-1, keepdims=True) + NORM_EPS) * g


@functools.lru_cache(maxsize=None)
def _chunk_consts():
    C = CHUNK
    t = np.arange(C)[:, None]
    i = np.arange(C)[None, :]
    tri = (i <= t).astype(np.float32)
    masks = []
    m = C // 2
    while m >= HG_BASE:
        same = (t // (2 * m)) == (i // (2 * m))
        masks.append((same & ((t % (2 * m)) >= m) & ((i % (2 * m)) < m)).astype(np.float32))
        m //= 2
    masks.append((((t // HG_BASE) == (i // HG_BASE)) & (i <= t)).astype(np.float32))
    hmask = np.stack(masks, 0)
    rmask = np.stack([(i < t).astype(np.float32), (i <= t).astype(np.float32)], 0)
    ones_bd = np.kron(np.eye(HEADS, dtype=np.float32), np.ones((HD, HD), np.float32))
    hmask = np.tile(hmask, (1, 1, 4))
    rmask = np.tile(rmask, (1, 1, 4))
    lane_unit = np.arange(4 * HD)[None, :] // HD
    gmask = np.stack([np.broadcast_to(lane_unit == g, (C, 4 * HD)) for g in range(4)], 0).astype(np.float32)
    bdmask = np.kron(np.eye(4, dtype=np.float32), np.ones((HD, HD), np.float32))
    return tri, hmask, rmask, ones_bd, gmask, bdmask


N_LEVELS = 4


def _dense_in_kernel(x_ref, g_ref, w_ref, z_ref):
    h = _rmsnorm(x_ref[...], g_ref[...])
    z_ref[...] = jnp.dot(h.astype(BF16), w_ref[...], preferred_element_type=F32)


def _layer(a, l):
    return pl.BlockSpec((None,) + a.shape[1:], lambda i, _n=a.ndim - 1: (l,) + (0,) * _n)


def _whole(a):
    return pl.BlockSpec(a.shape, lambda i, _n=a.ndim: (0,) * _n)


def _dense_in(x, g, w, l):
    rows = x.shape[0]
    tile = min(ROW_TILE, rows)
    return pl.pallas_call(
        _dense_in_kernel,
        out_shape=jax.ShapeDtypeStruct((rows, D_IN), F32),
        grid=(rows // tile,),
        in_specs=[pl.BlockSpec((tile, D_MODEL), lambda i: (i, 0)), _layer(g, l), _layer(w, l)],
        out_specs=pl.BlockSpec((tile, D_IN), lambda i: (i, 0)),
        compiler_params=pltpu.CompilerParams(
            dimension_semantics=("arbitrary",), vmem_limit_bytes=TPU_V7X_VMEM_LIMIT),
        name="dense_in",
    )(x, g, w)


def _dense_out_kernel(x_ref, o_ref, p_ref, wo_ref, gg_ref, wg_ref, wp_ref, gf_ref, y_ref, *, final):
    x1 = x_ref[...] + jnp.dot(o_ref[...], wo_ref[...], preferred_element_type=F32)
    gate = _sigmoid(jnp.dot(_rmsnorm(x1, gg_ref[...]).astype(BF16), wg_ref[...], preferred_element_type=F32))
    x2 = x1 + jnp.dot(p_ref[...].astype(BF16), wp_ref[...], preferred_element_type=F32) * gate
    if final:
        x2 = _rmsnorm(x2, gf_ref[...])
    y_ref[...] = x2


def _dense_out(x, o, p, wo, gg, wg, wp, gf, l, final):
    rows = x.shape[0]
    tile = min(ROW_TILE, rows)
    row_spec = lambda n: pl.BlockSpec((tile, n), lambda i: (i, 0))
    p_spec = pl.BlockSpec((None, tile, D_PLE), lambda i: (l, i, 0))
    return pl.pallas_call(
        functools.partial(_dense_out_kernel, final=final),
        out_shape=jax.ShapeDtypeStruct((rows, D_MODEL), F32),
        grid=(rows // tile,),
        in_specs=[row_spec(D_MODEL), row_spec(D_MIX), p_spec, _layer(wo, l), _layer(gg, l), _layer(wg, l),
                  _layer(wp, l), _whole(gf)],
        out_specs=row_spec(D_MODEL),
        compiler_params=pltpu.CompilerParams(
            dimension_semantics=("arbitrary",), vmem_limit_bytes=TPU_V7X_VMEM_LIMIT),
        name="dense_out",
    )(x, o, p, wo, gg, wg, wp, gf)


def _dense_mid_kernel(x_ref, o_ref, p_ref, wo_ref, gg_ref, wg_ref, wp_ref, gn_ref, wn_ref, y_ref, z_ref):
    x1 = x_ref[...] + jnp.dot(o_ref[...], wo_ref[...], preferred_element_type=F32)
    gate = _sigmoid(jnp.dot(_rmsnorm(x1, gg_ref[...]).astype(BF16), wg_ref[...], preferred_element_type=F32))
    x2 = x1 + jnp.dot(p_ref[...].astype(BF16), wp_ref[...], preferred_element_type=F32) * gate
    y_ref[...] = x2
    z_ref[...] = jnp.dot(_rmsnorm(x2, gn_ref[...]).astype(BF16), wn_ref[...], preferred_element_type=F32)


def _dense_mid(x, o, p, wo, gg, wg, wp, g_in, w_in, l):
    rows = x.shape[0]
    tile = min(ROW_TILE, rows)
    row_spec = lambda n: pl.BlockSpec((tile, n), lambda i: (i, 0))
    p_spec = pl.BlockSpec((None, tile, D_PLE), lambda i: (l, i, 0))
    return pl.pallas_call(
        _dense_mid_kernel,
        out_shape=(jax.ShapeDtypeStruct((rows, D_MODEL), F32), jax.ShapeDtypeStruct((rows, D_IN), F32)),
        grid=(rows // tile,),
        in_specs=[row_spec(D_MODEL), row_spec(D_MIX), p_spec, _layer(wo, l), _layer(gg, l), _layer(wg, l),
                  _layer(wp, l), _layer(g_in, l + 1), _layer(w_in, l + 1)],
        out_specs=(row_spec(D_MODEL), row_spec(D_IN)),
        compiler_params=pltpu.CompilerParams(
            dimension_semantics=("arbitrary",), vmem_limit_bytes=TPU_V7X_VMEM_LIMIT),
        name="dense_mid",
    )(x, o, p, wo, gg, wg, wp, g_in, w_in)


def _s5_scan(bu_ref, sall_ref, lre, lim, s_re, s_im, nt, nb):
    lr = jnp.broadcast_to(lre, (nb, D_S5_STATE))
    li = jnp.broadcast_to(lim, (nb, D_S5_STATE))

    def step(t, carry):
        sr, si = carry
        r0 = pl.multiple_of(t * nb, nb)
        br = bu_ref[pl.ds(r0, nb), 0:D_S5_STATE]
        bi = bu_ref[pl.ds(r0, nb), D_S5_STATE:2 * D_S5_STATE]
        nr = lr * sr - li * si + br
        ni = lr * si + li * sr + bi
        sall_ref[pl.ds(r0, nb), 0:D_S5_STATE] = nr
        sall_ref[pl.ds(r0, nb), D_S5_STATE:2 * D_S5_STATE] = ni
        return nr, ni

    return lax.fori_loop(0, nt, step, (s_re, s_im))


def _s5_head(u, sall_ref, cblk_ref, d_ref, g1_ref, g2_ref):
    y = jnp.dot(sall_ref[...].astype(BF16), cblk_ref[...], preferred_element_type=F32) + d_ref[...] * u
    y = jax.nn.gelu(y)
    yb = y.astype(BF16)
    return jnp.dot(yb, g1_ref[...], preferred_element_type=F32) * _sigmoid(
        jnp.dot(yb, g2_ref[...], preferred_element_type=F32))


PACK = 4
PW = PACK * HD
SEQ_PAIRS = 2
S5_STEPS = 16


def _interleave(required, background=()):
    live = list(required)
    extra = list(background)
    while live:
        for entry in list(live) + list(extra):
            gen, k = entry
            for _ in range(k):
                try:
                    next(gen)
                except StopIteration:
                    (live if entry in live else extra).remove(entry)
                    break


def _mix_prompt_packed_kernel(z_ref, lb_ref, hgn_ref, mu_ref, w0_ref, a0_ref, kk_ref, ka_ref, rk_ref, gng_ref,
                              gnb_ref, lre_ref, lim_ref, d_ref, wup_ref, aup_ref, bblk_ref, cblk_ref, g1_ref, g2_ref,
                              tri_ref, hmask_ref, rmask_ref, ones_ref, gmask_ref, bdmask_ref,
                              o_ref, hg_ref, rw_ref, sh_ref, sre_ref, sim_ref,
                              hg_bd0, hg_bd1, hg_bd2, rw_bd0, rw_bd1, rw_bd2, upad, up, bu, sall, ypad):
    nb = z_ref.shape[0]
    C = CHUNK
    n_pairs = D_HEADS // 128
    hg_bd = (hg_bd0, hg_bd1, hg_bd2)
    rw_bd = (rw_bd0, rw_bd1, rw_bd2)
    step = pl.program_id(0)

    @pl.when(step == 0)
    def _():
        for ref in hg_bd + rw_bd:
            ref[...] = jnp.zeros(ref.shape, F32)
        sh_ref[...] = jnp.zeros(sh_ref.shape, F32)
        sre_ref[...] = jnp.zeros(sre_ref.shape, F32)
        sim_ref[...] = jnp.zeros(sim_ref.shape, F32)

    ones_bd = ones_ref[...]
    bdmask = bdmask_ref[...]
    row_is0 = lax.broadcasted_iota(jnp.int32, (C, D_SHIFT), 0) == 0
    low_half = lax.broadcasted_iota(jnp.int32, (C, D_HEADS), 0) % 8 < HG_BASE

    def bd(x):
        m = x.shape[1] // PW
        blocks = []
        for g in range(PACK):
            mask = gmask_ref[g]
            if m > 1:
                mask = jnp.concatenate([mask] * m, axis=1)
            blocks.append(x * mask)
        return jnp.concatenate(blocks, axis=0)

    def prep_stages(b, p):
        q = z_ref[b, :, C_Q:C_Q + D_HEADS]
        f = z_ref[b, :, C_F:C_F + D_HEADS]
        lbs = jnp.maximum(lb_ref[...], LB_FLOOR)
        logf = jnp.log(lbs + (1.0 - lbs) * _sigmoid(f))
        kf = (1.0 - lbs) * _sigmoid(-f)
        qf = _silu(q)
        yield
        cum = _sel_left(tri_ref[...], logf)
        p["qin"] = (qf * jnp.exp(cum)).astype(BF16)
        p["kend"] = (kf * jnp.exp(cum[C - 1:C, :] - cum)).astype(BF16)
        yield
        for l in range(N_LEVELS):
            m = C >> (l + 1)
            ref = jnp.concatenate([jnp.broadcast_to(cum[q0 + m - 1:q0 + m, :], (2 * m, D_HEADS))
                                   for q0 in range(0, C, 2 * m)], axis=0)
            e = jnp.exp(-jnp.abs(cum - ref))
            p["qe%d" % l] = (qf * e).astype(BF16)
            p["ke%d" % l] = (kf * e).astype(BF16)
            yield
        first = jnp.concatenate([jnp.broadcast_to(cum[q0:q0 + 1, :], (8, D_HEADS)) for q0 in range(0, C, 8)], axis=0)
        second = jnp.concatenate([jnp.broadcast_to(cum[q0 + HG_BASE:q0 + HG_BASE + 1, :], (8, D_HEADS))
                                  for q0 in range(0, C, 8)], axis=0)
        dq = cum - jnp.where(low_half, first, second)
        p["qe%d" % N_LEVELS] = (qf * jnp.exp(dq)).astype(BF16)
        p["ke%d" % N_LEVELS] = (kf * jnp.exp(-dq)).astype(BF16)
        p["gcol"] = jnp.exp(cum[C - 8:C, :].T)[:, 7:8]
        p["ib"] = z_ref[b, :, C_I:C_I + D_HEADS].astype(BF16)
        yield

        c = z_ref[b, :, C_RW:C_RW + D_SHIFT]
        prev = jnp.where(row_is0, sh_ref[pl.ds(b, 1), :], pltpu.roll(c, 1, axis=0))
        p["last"] = c[C - 1:C, :]
        cs = c + mu_ref[...] * (prev - c)
        r = cs[:, 0:D_HEADS]
        k = cs[:, D_HEADS:2 * D_HEADS]
        v = cs[:, 2 * D_HEADS:3 * D_HEADS]
        wd = cs[:, 3 * D_HEADS:3 * D_HEADS + LORA]
        ad = cs[:, 3 * D_HEADS + LORA:D_SHIFT]
        logw = -RW_DECAY_SCALE * _sigmoid(w0_ref[...] + _dot(jnp.tanh(wd), wup_ref[...]))
        a = _sigmoid(a0_ref[...] + _dot(ad, aup_ref[...]))
        yield
        kk = k * kk_ref[...]
        kk = kk * lax.rsqrt(jnp.maximum(_head_sum(kk * kk, ones_bd), 1e-24))
        kt = k * (1.0 + (a - 1.0) * ka_ref[...])
        beta = kk * a
        p["bonus"] = _head_sum(r * kt * rk_ref[...], ones_bd)
        cl = _sel_left(tri_ref[...], logw)
        yield
        p["aq"] = kk * jnp.exp(cl - logw)
        p["rq"] = r * jnp.exp(cl)
        ecl = jnp.exp(-cl)
        p["kd"] = (kt * ecl).astype(BF16)
        p["bd"] = (beta * ecl).astype(BF16)
        eend = jnp.exp(cl[C - 1:C, :] - cl)
        p["ktl"] = kt * eend
        p["btl"] = beta * eend
        p["gam"] = jnp.exp(cl[C - 1:C, :])
        p["v"] = v

    def chain_stages(i, ps):
        b0 = 2 * SEQ_PAIRS * i
        pairs = range(SEQ_PAIRS * n_pairs)
        pj = [u % n_pairs for u in pairs]
        pq = [2 * (u // n_pairs) for u in pairs]
        tiles = [slice(pj[u] * 128, (pj[u] + 1) * 128) for u in pairs]
        slot = [SEQ_PAIRS * i + u // n_pairs for u in pairs]

        def x4(name, rows=False):
            if rows:
                return [jnp.concatenate([ps[pq[u]][name][tiles[u], :], ps[pq[u] + 1][name][tiles[u], :]], axis=0)
                        for u in pairs]
            return [jnp.concatenate([ps[pq[u]][name][:, tiles[u]], ps[pq[u] + 1][name][:, tiles[u]]], axis=1)
                    for u in pairs]

        s_hg = [hg_bd[pj[u]][slot[u]] for u in pairs]
        s_rw = [rw_bd[pj[u]][slot[u]] for u in pairs]
        aq4, rq4, v4, btl4, ktl4 = x4("aq"), x4("rq"), x4("v"), x4("btl"), x4("ktl")
        bd4, kd4 = x4("bd"), x4("kd")
        ar = [jnp.concatenate([aq4[j], rq4[j]], axis=0).astype(BF16) for j in pairs]
        zb = [_dot_nt(ar[j], bd(bd4[j])) for j in pairs]
        zk = [_dot_nt(ar[j], bd(kd4[j])) for j in pairs]
        yield
        mb = [zb[j][0:C] * rmask_ref[0] for j in pairs]
        nbm = [zb[j][C:2 * C] * rmask_ref[1] for j in pairs]
        mk = [zk[j][0:C] * rmask_ref[0] for j in pairs]
        nk = [zk[j][C:2 * C] * rmask_ref[1] for j in pairs]
        bdv = [bd(v4[j].astype(BF16)) for j in pairs]
        x0 = [jnp.concatenate([aq4[j], _dot(mk[j], bdv[j])], axis=1) for j in pairs]
        yield
        bm = [-m for m in mb]
        pw = [_dot(mb[j], bd(mb[j].astype(BF16))) for j in pairs]
        att = [None for _ in pairs]
        level = 0
        n = 2
        while 2 * n < C:
            yield
            both = [_dot(jnp.concatenate([bm[j], pw[j]], axis=0), bd(pw[j].astype(BF16))) for j in pairs]
            bm = [bm[j] + pw[j] + both[j][0:C] for j in pairs]
            pw = [both[j][C:2 * C] for j in pairs]
            n *= 2
            qe4, ke4 = x4("qe%d" % level), x4("ke%d" % level)
            for j in pairs:
                term = hmask_ref[level] * _dot_nt(qe4[j], bd(ke4[j]))
                att[j] = term if att[j] is None else att[j] + term
            level += 1
        yield
        bm = [bm[j] + pw[j] + _dot(bm[j], bd(pw[j].astype(BF16))) for j in pairs]
        while level <= N_LEVELS:
            qe4, ke4 = x4("qe%d" % level), x4("ke%d" % level)
            for j in pairs:
                term = hmask_ref[level] * _dot_nt(qe4[j], bd(ke4[j]))
                att[j] = term if att[j] is None else att[j] + term
            level += 1
        yield
        ib4, qin4, kend4 = x4("ib"), x4("qin"), x4("kend")
        o4 = [_dot(att[j], bd(ib4[j])) + _dot(qin4[j], s_hg[j]) for j in pairs]
        gcol4 = x4("gcol", rows=True)
        new_hg = [gcol4[j] * s_hg[j] + bdmask * _dot_tn(kend4[j], ib4[j]) for j in pairs]
        x = [x0[j] + _dot(bm[j], bd(x0[j].astype(BF16))) for j in pairs]
        yield
        bdx = [bd(x[j].astype(BF16)) for j in pairs]
        w = [bdmask * _dot_tn(btl4[j], x[j][:, 0:PW]) for j in pairs]
        gt = [bdmask * _dot_tn(jnp.concatenate([v4[j], x[j][:, PW:2 * PW]], axis=0),
                               jnp.concatenate([ktl4[j], -btl4[j]], axis=0)) for j in pairs]
        nx = [_dot(nbm[j], bdx[j]) for j in pairs]
        yield
        y4 = [_dot_nt(rq4[j] - nx[j][:, 0:PW], s_rw[j]) + _dot(nk[j], bdv[j]) - nx[j][:, PW:2 * PW] for j in pairs]
        gam4 = x4("gam")
        new_rw = [s_rw[j] * gam4[j] - _dot_nt(s_rw[j], w[j]) + gt[j] for j in pairs]
        for j in pairs:
            hg_bd[pj[j]][slot[j]] = new_hg[j]
            rw_bd[pj[j]][slot[j]] = new_rw[j]
        yield
        o_tiles = [[o4[(u // 2) * n_pairs + j][:, (u % 2) * 128:(u % 2 + 1) * 128] for j in range(n_pairs)]
                   for u in range(2 * SEQ_PAIRS)]
        y_tiles = [[y4[(u // 2) * n_pairs + j][:, (u % 2) * 128:(u % 2 + 1) * 128] for j in range(n_pairs)]
                   for u in range(2 * SEQ_PAIRS)]
        for u in range(2 * SEQ_PAIRS):
            b = b0 + u
            o = jnp.concatenate(o_tiles[u], axis=1)
            ssq = _head_sum(o * o, ones_bd)
            o = o * lax.rsqrt(ssq * (1.0 / HD) + NORM_EPS) * hgn_ref[...]
            o_ref[b, :, 0:D_HEADS] = (o * _silu(z_ref[b, :, C_GA:C_GA + D_HEADS])).astype(BF16)
            y = jnp.concatenate(y_tiles[u], axis=1)
            yc = y - _head_sum(y, ones_bd) * (1.0 / HD)
            var = _head_sum(yc * yc, ones_bd) * (1.0 / HD)
            yn = yc * lax.rsqrt(var + RW_GN_EPS) * gng_ref[...] + gnb_ref[...] + ps[u]["bonus"] * ps[u]["v"]
            o_ref[b, :, D_HEADS:2 * D_HEADS] = (yn * _silu(z_ref[b, :, C_GB:C_GB + D_HEADS])).astype(BF16)
            sh_ref[pl.ds(b, 1), :] = ps[u]["last"]
            yield

    n_lane = D_S5 // 128
    n_groups = nb // (2 * SEQ_PAIRS)
    s5_blocks = C // S5_STEPS // n_groups
    blk_rows = S5_STEPS * nb

    def s5_stages(i, state):
        lr = jnp.broadcast_to(lre_ref[...], (nb, D_S5_STATE))
        li = jnp.broadcast_to(lim_ref[...], (nb, D_S5_STATE))
        sr, si = state
        for k in range(s5_blocks):
            t0 = (i * s5_blocks + k) * S5_STEPS
            row0 = pl.multiple_of(t0 * nb, blk_rows)
            for tt in range(S5_STEPS):
                for j in range(n_lane):
                    up[pl.ds(pl.multiple_of(row0 + tt * nb, nb), nb), j * 128:(j + 1) * 128] = (
                        upad[j, pl.ds(t0 + tt, nb, stride=PITCH), :])
            u = up[pl.ds(row0, blk_rows), :]
            bu[pl.ds(row0, blk_rows), :] = jnp.dot(u.astype(BF16), bblk_ref[...], preferred_element_type=F32)
            yield
            for tt in range(S5_STEPS):
                r0 = pl.multiple_of(row0 + tt * nb, nb)
                br = bu[pl.ds(r0, nb), 0:D_S5_STATE]
                bi = bu[pl.ds(r0, nb), D_S5_STATE:2 * D_S5_STATE]
                sr, si = lr * sr - li * si + br, lr * si + li * sr + bi
                sall[pl.ds(r0, nb), 0:D_S5_STATE] = sr
                sall[pl.ds(r0, nb), D_S5_STATE:2 * D_S5_STATE] = si
                if tt % 4 == 3:
                    yield
            y = jnp.dot(sall[pl.ds(row0, blk_rows), :].astype(BF16), cblk_ref[...],
                        preferred_element_type=F32) + d_ref[...] * u
            yb = jax.nn.gelu(y).astype(BF16)
            yg = jnp.dot(yb, g1_ref[...], preferred_element_type=F32) * _sigmoid(
                jnp.dot(yb, g2_ref[...], preferred_element_type=F32))
            for tt in range(S5_STEPS):
                for j in range(n_lane):
                    ypad[j, pl.ds(t0 + tt, nb, stride=PITCH), :] = yg[tt * nb:(tt + 1) * nb, j * 128:(j + 1) * 128]
            yield
        state[0], state[1] = sr, si

    def prep_group(i, ps):
        for q in range(2 * SEQ_PAIRS):
            yield from prep_stages(2 * SEQ_PAIRS * i + q, ps[q])

    for b in range(nb):
        for j in range(n_lane):
            upad[j, b * PITCH:b * PITCH + C, :] = z_ref[b, :, C_U + j * 128:C_U + (j + 1) * 128]

    def group_body(i, carry):
        ps = [{} for _ in range(2 * SEQ_PAIRS)]
        state = list(carry)
        s5 = (s5_stages(i, state), 1)
        _interleave([(prep_group(i, ps), 1)])
        _interleave([(chain_stages(i, ps), 1)], background=[s5])
        _interleave([s5])
        return tuple(state)

    s_re, s_im = lax.fori_loop(0, n_groups, group_body, (sre_ref[...], sim_ref[...]))
    sre_ref[...] = s_re
    sim_ref[...] = s_im
    for b in range(nb):
        for j in range(n_lane):
            gate = _silu(z_ref[b, :, C_GC + j * 128:C_GC + (j + 1) * 128])
            o_ref[b, :, 2 * D_HEADS + j * 128:2 * D_HEADS + (j + 1) * 128] = (
                ypad[j, b * PITCH:b * PITCH + C, :] * gate).astype(BF16)

    @pl.when(step == pl.num_programs(0) - 1)
    def _():
        for i in range(nb // 2):
            for j in range(n_pairs):
                for g in range(PACK):
                    b = 2 * i + g // 2
                    h = 2 * j + g % 2
                    blk = slice(g * HD, (g + 1) * HD)
                    hg_ref[b, h] = hg_bd[j][i, blk, :][:, blk]
                    rw_ref[b, h] = rw_bd[j][i, blk, :][:, blk]


def _mix_prompt(z, vecs, mats, consts, l):
    B, L, _ = z.shape
    nt = L // CHUNK
    out_shape = (
        jax.ShapeDtypeStruct((B, L, D_MIX), BF16),
        jax.ShapeDtypeStruct((B, HEADS, HD, HD), F32),
        jax.ShapeDtypeStruct((B, HEADS, HD, HD), F32),
        jax.ShapeDtypeStruct((B, D_SHIFT), F32),
        jax.ShapeDtypeStruct((B, D_S5_STATE), F32),
        jax.ShapeDtypeStruct((B, D_S5_STATE), F32),
    )
    out_specs = (
        pl.BlockSpec((B, CHUNK, D_MIX), lambda i: (0, i, 0)),
        pl.BlockSpec((B, HEADS, HD, HD), lambda i: (0, 0, 0, 0)),
        pl.BlockSpec((B, HEADS, HD, HD), lambda i: (0, 0, 0, 0)),
        pl.BlockSpec((B, D_SHIFT), lambda i: (0, 0)),
        pl.BlockSpec((B, D_S5_STATE), lambda i: (0, 0)),
        pl.BlockSpec((B, D_S5_STATE), lambda i: (0, 0)),
    )
    args = (z,) + tuple(vecs) + tuple(mats) + tuple(consts)
    in_specs = ([pl.BlockSpec((B, CHUNK, D_IN), lambda i: (0, i, 0))]
                + [_layer(a, l) for a in tuple(vecs) + tuple(mats)] + [_whole(a) for a in consts])
    rows = B * CHUNK
    assert B % (2 * SEQ_PAIRS) == 0 and D_HEADS // 128 == 3
    scratch = [pltpu.VMEM((B // 2, PW, PW), F32) for _ in range(6)] + [
        pltpu.VMEM((D_S5 // 128, B * PITCH, 128), F32),
        pltpu.VMEM((rows, D_S5), F32),
        pltpu.VMEM((rows, 2 * D_S5_STATE), F32),
        pltpu.VMEM((rows, 2 * D_S5_STATE), F32),
        pltpu.VMEM((D_S5 // 128, B * PITCH, 128), F32),
    ]
    return pl.pallas_call(
        _mix_prompt_packed_kernel,
        out_shape=out_shape,
        grid=(nt,),
        in_specs=in_specs,
        out_specs=out_specs,
        scratch_shapes=scratch,
        compiler_params=pltpu.CompilerParams(
            dimension_semantics=("arbitrary",), vmem_limit_bytes=TPU_V7X_VMEM_LIMIT),
        name="mix_prompt",
    )(*args)


def _mix_sample_lanes_kernel(zq, zf, zi, zga, zr, zk, zv, zwd, zad, zgb, pr, pk, pv, pwd, pad_,
                             mur, muk, muv, muwd, muad, lb_ref, hgn_ref, w0_ref, a0_ref, kk_ref, ka_ref, rk_ref,
                             gng_ref, gnb_ref, wupT_ref, aupT_ref, hg_in, rw_in, hg_prev, rw_prev,
                             oa_ref, ob_ref, hg_ref, rw_ref,
                             fg_s, kf_s, qf_s, iv_s, kk_s, dec_s, beta_s, kt_s, r_s, v_s, y_s):
    del hg_prev, rw_prev
    nb = pr.shape[1]
    T = zq.shape[1] // nb
    lbs = jnp.maximum(lb_ref[...], LB_FLOOR)
    col = lambda ref, t: ref[:, t * nb:(t + 1) * nb]

    def shifted(z, p, mu, t):
        c = col(z, t)
        prev = p[...] if t == 0 else col(z, t - 1)
        return c + mu[...] * (prev - c)

    for t in range(T):
        f = col(zf, t)
        fg_s[t] = lbs + (1.0 - lbs) * _sigmoid(f)
        kf_s[t] = (1.0 - lbs) * _sigmoid(-f)
        qf_s[t] = _silu(col(zq, t))
        iv_s[t] = col(zi, t)
        r = shifted(zr, pr, mur, t)
        k = shifted(zk, pk, muk, t)
        v = shifted(zv, pv, muv, t)
        wd = shifted(zwd, pwd, muwd, t)
        ad = shifted(zad, pad_, muad, t)
        logw = -RW_DECAY_SCALE * _sigmoid(w0_ref[...] + _dot(wupT_ref[...], jnp.tanh(wd)))
        a = _sigmoid(a0_ref[...] + _dot(aupT_ref[...], ad))
        kk = k * kk_ref[...]
        kk = kk * lax.rsqrt(jnp.maximum(jnp.sum(kk * kk, axis=0, keepdims=True), 1e-24))
        kk_s[t] = kk
        dec_s[t] = jnp.exp(logw)
        beta_s[t] = kk * a
        kt_s[t] = k * (1.0 + (a - 1.0) * ka_ref[...])
        r_s[t] = r
        v_s[t] = v

    def hg_row(k, acc):
        s = hg_in[k]
        acc = list(acc)
        for t in range(T):
            row = lambda ref: ref[t, pl.ds(k, 1), :]
            s = row(fg_s) * s + row(kf_s) * iv_s[t]
            acc[t] = acc[t] + row(qf_s) * s
        hg_ref[k] = s
        return tuple(acc)

    zero = jnp.zeros((HD, nb), F32)
    o = lax.fori_loop(0, HD, hg_row, (zero,) * T)
    for t in range(T):
        ssq = jnp.sum(o[t] * o[t], axis=0, keepdims=True)
        on = o[t] * lax.rsqrt(ssq * (1.0 / HD) + NORM_EPS) * hgn_ref[...]
        oa_ref[:, t * nb:(t + 1) * nb] = (on * _silu(col(zga, t))).astype(BF16)

    def rw_row(vi, carry):
        s = rw_in[vi]
        for t in range(T):
            sa = jnp.sum(s * kk_s[t], axis=0, keepdims=True)
            s = s * dec_s[t] - sa * beta_s[t] + v_s[t, pl.ds(vi, 1), :] * kt_s[t]
            y_s[t, pl.ds(vi, 1), :] = jnp.sum(s * r_s[t], axis=0, keepdims=True)
        rw_ref[vi] = s
        return carry

    lax.fori_loop(0, HD, rw_row, 0, unroll=4)
    for t in range(T):
        y = y_s[t]
        yc = y - jnp.mean(y, axis=0, keepdims=True)
        var = jnp.mean(yc * yc, axis=0, keepdims=True)
        bonus = jnp.sum(r_s[t] * kt_s[t] * rk_ref[...], axis=0, keepdims=True)
        yn = yc * lax.rsqrt(var + RW_GN_EPS) * gng_ref[...] + gnb_ref[...] + bonus * v_s[t]
        ob_ref[:, t * nb:(t + 1) * nb] = (yn * _silu(col(zgb, t))).astype(BF16)


def _mix_sample_lanes(zT, shT, hgT, rwT, carried, cols, mu_cols, wupT, aupT, l):
    B = shT.shape[2]
    TB = zT.shape[1]
    zblk = lambda off: pl.BlockSpec((HD, TB), lambda h: (off // HD + h, 0))
    zone = lambda off: pl.BlockSpec((HD, TB), lambda h: (off // HD, 0))
    sblk = lambda off: pl.BlockSpec((None, HD, B), lambda h: (l, off // HD + h, 0))
    sone = lambda off: pl.BlockSpec((None, HD, B), lambda h: (l, off // HD, 0))
    r0, k0, v0 = 0, D_HEADS, 2 * D_HEADS
    wd0, ad0 = 3 * D_HEADS, 3 * D_HEADS + LORA
    state = pl.BlockSpec((None, None, HD, HD, B), lambda h: (l, h, 0, 0, 0))
    head_cols = pl.BlockSpec((None, HD, B), lambda h: (l, h, 0))
    lora = pl.BlockSpec((None, HD, LORA), lambda h: (l, h, 0))
    in_specs = ([zblk(C_Q), zblk(C_F), zblk(C_I), zblk(C_GA), zblk(C_RW + r0), zblk(C_RW + k0), zblk(C_RW + v0),
                 zone(C_RW + wd0), zone(C_RW + ad0), zblk(C_GB)]
                + [sblk(r0), sblk(k0), sblk(v0), sone(wd0), sone(ad0)]
                + [sblk(r0), sblk(k0), sblk(v0), sone(wd0), sone(ad0)]
                + [head_cols] * len(cols) + [lora, lora, state, state]
                + [pl.BlockSpec(memory_space=pl.ANY)] * 2)
    out_row = pl.BlockSpec((HD, TB), lambda h: (h, 0))
    slab = pltpu.VMEM((TB // B, HD, B), F32)
    n_in = len(in_specs)
    return pl.pallas_call(
        _mix_sample_lanes_kernel,
        out_shape=(
            jax.ShapeDtypeStruct((D_HEADS, TB), BF16),
            jax.ShapeDtypeStruct((D_HEADS, TB), BF16),
            jax.ShapeDtypeStruct(hgT.shape, F32),
            jax.ShapeDtypeStruct(rwT.shape, F32),
        ),
        grid=(HEADS,),
        in_specs=in_specs,
        out_specs=(out_row, out_row, state, state),
        input_output_aliases={n_in - 2: 2, n_in - 1: 3},
        scratch_shapes=[slab] * 11,
        compiler_params=pltpu.CompilerParams(
            dimension_semantics=("arbitrary",), vmem_limit_bytes=TPU_V7X_VMEM_LIMIT),
        name="mix_sample",
    )(*([zT] * 10 + [shT] * 5 + [mu_cols] * 5 + list(cols) + [wupT, aupT, hgT, rwT] + list(carried)))


def _s5_sample_kernel(z_ref, sre_in, sim_in, lre_ref, lim_ref, d_ref, bblk_ref, cblk_ref, g1_ref, g2_ref,
                      o_ref, sre_ref, sim_ref, bu, sall, *, nt, nb):
    u = z_ref[:, C_U:C_U + D_S5]
    bu[...] = jnp.dot(u.astype(BF16), bblk_ref[...], preferred_element_type=F32)
    s_re, s_im = _s5_scan(bu, sall, lre_ref[...], lim_ref[...], sre_in[...], sim_in[...], nt, nb)
    sre_ref[...] = s_re
    sim_ref[...] = s_im
    yg = _s5_head(u, sall, cblk_ref, d_ref, g1_ref, g2_ref)
    o_ref[...] = (yg * _silu(z_ref[:, C_GC:C_GC + D_S5])).astype(BF16)


def _s5_sample(z2d, sre, sim, stacked, nt, nb, l):
    rows = z2d.shape[0]
    return pl.pallas_call(
        functools.partial(_s5_sample_kernel, nt=nt, nb=nb),
        out_shape=(
            jax.ShapeDtypeStruct((rows, D_S5), BF16),
            jax.ShapeDtypeStruct(sre.shape, F32),
            jax.ShapeDtypeStruct(sim.shape, F32),
        ),
        grid=(1,),
        in_specs=[_whole(z2d), _layer(sre, l), _layer(sim, l)] + [_layer(a, l) for a in stacked],
        out_specs=(pl.BlockSpec((rows, D_S5), lambda i: (0, 0)), _layer(sre, l), _layer(sim, l)),
        input_output_aliases={1: 1, 2: 2},
        scratch_shapes=[pltpu.VMEM((rows, 2 * D_S5_STATE), F32), pltpu.VMEM((rows, 2 * D_S5_STATE), F32)],
        compiler_params=pltpu.CompilerParams(
            dimension_semantics=("arbitrary",), vmem_limit_bytes=TPU_V7X_VMEM_LIMIT),
        name="s5_sample",
    )(z2d, sre, sim, *stacked)


def _prep_kernel(lbraw_ref, are_ref, aim_ref, ldt_ref, bre_ref, bim_ref, lb_ref, lre_ref, lim_ref, bbre_ref, bbim_ref):
    raw = lbraw_ref[...]
    e = jnp.exp(raw - jnp.max(raw, axis=0, keepdims=True))
    sm = e / jnp.sum(e, axis=0, keepdims=True)
    depth = raw.shape[0]
    acc = jnp.zeros_like(sm[0:1])
    for l in range(depth):
        acc = acc + sm[l:l + 1]
        lb_ref[l:l + 1, :] = acc - sm[0:1]
    a_re = are_ref[...]
    a_im = aim_ref[...]
    dt = jnp.exp(ldt_ref[...])
    mag = jnp.exp(dt * a_re)
    lam_re = mag * jnp.cos(dt * a_im)
    lam_im = mag * jnp.sin(dt * a_im)
    den = a_re * a_re + a_im * a_im
    xr = lam_re - 1.0
    f_re = (xr * a_re + lam_im * a_im) / den
    f_im = (lam_im * a_re - xr * a_im) / den
    lre_ref[...] = lam_re
    lim_ref[...] = lam_im
    b_re = bre_ref[...]
    b_im = bim_ref[...]
    bbre_ref[...] = f_re * b_re - f_im * b_im
    bbim_ref[...] = f_re * b_im + f_im * b_re


def _prep(hg_lb_raw, s5_a_re, s5_a_im, s5_log_dt, s5_b_re, s5_b_im):
    depth = hg_lb_raw.shape[0]
    n = depth * S5_GROUPS * S5_P
    col = lambda a: a.reshape(n, 1)
    ldt = jnp.broadcast_to(s5_log_dt[:, :, None], (depth, S5_GROUPS, S5_P))
    lb, lre, lim, bbre, bbim = pl.pallas_call(
        _prep_kernel,
        out_shape=(
            jax.ShapeDtypeStruct((depth, D_HEADS), F32),
            jax.ShapeDtypeStruct((n, 1), F32),
            jax.ShapeDtypeStruct((n, 1), F32),
            jax.ShapeDtypeStruct((n, S5_CH), F32),
            jax.ShapeDtypeStruct((n, S5_CH), F32),
        ),
        name="param_prep",
    )(hg_lb_raw, col(s5_a_re), col(s5_a_im), col(ldt), s5_b_re.reshape(n, S5_CH), s5_b_im.reshape(n, S5_CH))
    shape3 = (depth, S5_GROUPS, S5_P)
    return (lb, lre.reshape(depth, 1, D_S5_STATE), lim.reshape(depth, 1, D_S5_STATE),
            bbre.reshape(shape3 + (S5_CH,)), bbim.reshape(shape3 + (S5_CH,)))


def kernel(x_prompt, x_sample, state_hgrn, state_rwkv, state_rwkv_shift, state_s5_re, state_s5_im, p_prompt, p_sample, g_in, w_in, hg_lb_raw, hg_norm_g, rw_mu, rw_w0, rw_w_up, rw_a0, rw_a_up, rw_k_k, rw_k_a, rw_r_k, rw_gn_g, rw_gn_b, s5_a_re, s5_a_im, s5_log_dt, s5_b_re, s5_b_im, s5_c_re, s5_c_im, s5_d, s5_glu_w1, s5_glu_w2, w_out, ple_w_proj, ple_gate_g, ple_w_gate, g_final):
    depth = w_in.shape[0]
    B, L, _ = x_prompt.shape
    BS, T, _ = x_sample.shape
    tri, hmask, rmask, ones_np, gmask, bdmask = _chunk_consts()
    ones_bd = jnp.asarray(ones_np, BF16)
    consts = (jnp.asarray(tri, BF16), jnp.asarray(hmask, F32), jnp.asarray(rmask, F32), ones_bd,
              jnp.asarray(gmask, BF16), jnp.asarray(bdmask, F32))

    lb_all, lam_re, lam_im, bb_re, bb_im = _prep(hg_lb_raw, s5_a_re, s5_a_im, s5_log_dt, s5_b_re, s5_b_im)
    g_final2 = g_final.reshape(1, -1)

    xp = x_prompt.reshape(B * L, D_MODEL)
    xs = jnp.transpose(x_sample, (1, 0, 2)).reshape(T * BS, D_MODEL)
    pp = p_prompt.reshape(depth, B * L, D_PLE)
    ps = jnp.transpose(p_sample, (0, 2, 1, 3)).reshape(depth, T * BS, D_PLE)

    rows3 = lambda a: a.reshape(depth, 1, -1)
    cast = lambda a: a.astype(BF16)
    g_in3, gate_g3 = rows3(g_in), rows3(ple_gate_g)
    w_in_b, wo, wg, wp = cast(w_in), cast(w_out), cast(ple_w_gate), cast(ple_w_proj)
    wup, aup, g1, g2 = cast(rw_w_up), cast(rw_a_up), cast(s5_glu_w1), cast(s5_glu_w2)
    eye = jnp.eye(S5_GROUPS, dtype=F32)
    bd_in = lambda bb: jnp.einsum('lgpc,gh->lgchp', bb, eye).reshape(depth, D_S5, D_S5_STATE)
    bd_out = lambda c: jnp.einsum('lgcp,gh->lgphc', c, eye).reshape(depth, D_S5_STATE, D_S5)
    bblk = cast(jnp.concatenate([bd_in(bb_re), bd_in(bb_im)], axis=2))
    cblk = cast(jnp.concatenate([bd_out(s5_c_re), -bd_out(s5_c_im)], axis=1))
    hr_vecs = tuple(rows3(a) for a in (lb_all, hg_norm_g, rw_mu, rw_w0, rw_a0, rw_k_k, rw_k_a, rw_r_k,
                                       rw_gn_g, rw_gn_b))
    s5_vecs = (lam_re, lam_im, rows3(s5_d))

    hg_t = jnp.transpose(state_hgrn, (0, 2, 3, 4, 1))
    rw_t = jnp.transpose(state_rwkv, (0, 2, 3, 4, 1))
    sh_t = jnp.transpose(state_rwkv_shift, (0, 2, 1))
    lanes = lambda a: jnp.broadcast_to(a[:, :, None], a.shape + (BS,))
    s_cols = tuple(lanes(a) for a in (lb_all, hg_norm_g, rw_w0, rw_a0, rw_k_k, rw_k_a, rw_r_k, rw_gn_g, rw_gn_b))
    mu_cols = lanes(rw_mu)
    wup_t = cast(jnp.transpose(rw_w_up, (0, 2, 1)))
    aup_t = cast(jnp.transpose(rw_a_up, (0, 2, 1)))
    carried = (jnp.zeros(hg_t.shape, F32), jnp.zeros(rw_t.shape, F32))
    sh_out = []
    sre_s = state_s5_re.reshape(depth, BS, D_S5_STATE)
    sim_s = state_s5_im.reshape(depth, BS, D_S5_STATE)
    outs_p = [[] for _ in range(5)]
    for l in range(depth):
        final = l == depth - 1

        if l == 0:
            z = _dense_in(xp, g_in3, w_in_b, l)
        o, hg, rw, sh, sre, sim = _mix_prompt(
            z.reshape(B, L, D_IN), hr_vecs + s5_vecs, (wup, aup, bblk, cblk, g1, g2), consts, l)
        if final:
            xp = _dense_out(xp, o.reshape(B * L, D_MIX), pp, wo, gate_g3, wg, wp, g_final2, l, final)
        else:
            xp, z = _dense_mid(xp, o.reshape(B * L, D_MIX), pp, wo, gate_g3, wg, wp, g_in3, w_in_b, l)
        for acc, val in zip(outs_p, (hg, rw, sh, sre.reshape(B, S5_GROUPS, S5_P), sim.reshape(B, S5_GROUPS, S5_P))):
            acc.append(val)

        zs = _dense_in(xs, g_in3, w_in_b, l)
        oa_t, ob_t, *carried = _mix_sample_lanes(zs.T, sh_t, hg_t, rw_t, tuple(carried), s_cols, mu_cols,
                                                 wup_t, aup_t, l)
        sh_out.append(zs[(T - 1) * BS:, C_RW:C_RW + D_SHIFT])
        o_c, sre_s, sim_s = _s5_sample(zs, sre_s, sim_s, s5_vecs + (bblk, cblk, g1, g2), T, BS, l)
        os_ = jnp.concatenate([oa_t.T, ob_t.T, o_c], axis=1)
        xs = _dense_out(xs, os_, ps, wo, gate_g3, wg, wp, g_final2, l, final)

    y_prompt = xp.reshape(B, L, D_MODEL)
    y_sample = jnp.transpose(xs.reshape(T, BS, D_MODEL), (1, 0, 2))
    s5_shape = (depth, BS, S5_GROUPS, S5_P)
    hg_s, rw_s = (jnp.transpose(a, (0, 4, 1, 2, 3)) for a in carried)
    return ((y_prompt, y_sample) + tuple(jnp.stack(a) for a in outs_p)
            + (hg_s, rw_s, jnp.stack(sh_out), sre_s.reshape(s5_shape), sim_s.reshape(s5_shape)))
```

```python
import functools
import math

import jax
import jax.numpy as jnp
import numpy as np
from jax import lax
from jax.experimental import pallas as pl
from jax.experimental.pallas import tpu as pltpu

F32 = jnp.float32
BF16 = jnp.bfloat16

D_MODEL = 1024
D_PLE = 256
HEADS = 6
HD = 64
D_HEADS = HEADS * HD
LORA = 64
D_SHIFT = 3 * D_HEADS + 2 * LORA
S5_GROUPS = 16
S5_CH = 16
S5_P = 64
D_S5 = S5_GROUPS * S5_CH
D_S5_STATE = S5_GROUPS * S5_P
D_IN = 4 * D_HEADS + D_SHIFT + D_HEADS + 2 * D_S5
D_MIX = 2 * D_HEADS + D_S5

C_Q, C_F, C_I, C_GA = 0, D_HEADS, 2 * D_HEADS, 3 * D_HEADS
C_RW = 4 * D_HEADS
C_GB = C_RW + D_SHIFT
C_U = C_GB + D_HEADS
C_GC = C_U + D_S5

LB_FLOOR = 1e-12
NORM_EPS = 1e-6
RW_GN_EPS = 64e-5
RW_DECAY_SCALE = math.exp(-0.5)

CHUNK = 64
HG_BASE = 4
ROW_TILE = 512
PITCH = CHUNK + 8
TPU_V7X_VMEM_LIMIT = 56 * 1024 * 1024


def _dot(a, b):
    return jnp.dot(a.astype(BF16), b.astype(BF16), preferred_element_type=F32)


def _dot_nt(a, b):
    return lax.dot_general(a.astype(BF16), b.astype(BF16), (((1,), (1,)), ((), ())), preferred_element_type=F32)


def _dot_tn(a, b):
    return lax.dot_general(a.astype(BF16), b.astype(BF16), (((0,), (0,)), ((), ())), preferred_element_type=F32)


def _split3(x):
    hi = x.astype(BF16)
    r1 = x - hi.astype(F32)
    mid = r1.astype(BF16)
    lo = (r1 - mid.astype(F32)).astype(BF16)
    return hi, mid, lo


def _sel_left(m, x):
    hi, mid, lo = _split3(x)
    d = lambda p: jnp.dot(m, p, preferred_element_type=F32)
    return d(hi) + d(mid) + d(lo)


def _head_sum(x, m):
    return jnp.dot(x.astype(BF16), m, preferred_element_type=F32)


def _sigmoid(x):
    return jax.nn.sigmoid(x)


def _silu(x):
    return x * jax.nn.sigmoid(x)


def _rmsnorm(x, g):
    return x * lax.rsqrt(jnp.mean(x * x, axis=-1, keepdims=True) + NORM_EPS) * g


@functools.lru_cache(maxsize=None)
def _chunk_consts():
    C = CHUNK
    t = np.arange(C)[:, None]
    i = np.arange(C)[None, :]
    tri = (i <= t).astype(np.float32)
    masks = []
    m = C // 2
    while m >= HG_BASE:
        same = (t // (2 * m)) == (i // (2 * m))
        masks.append((same & ((t % (2 * m)) >= m) & ((i % (2 * m)) < m)).astype(np.float32))
        m //= 2
    masks.append((((t // HG_BASE) == (i // HG_BASE)) & (i <= t)).astype(np.float32))
    hmask = np.stack(masks, 0)
    rmask = np.stack([(i < t).astype(np.float32), (i <= t).astype(np.float32)], 0)
    ones_bd = np.kron(np.eye(HEADS, dtype=np.float32), np.ones((HD, HD), np.float32))
    hmask = np.tile(hmask, (1, 1, 4))
    rmask = np.tile(rmask, (1, 1, 4))
    lane_unit = np.arange(4 * HD)[None, :] // HD
    gmask = np.stack([np.broadcast_to(lane_unit == g, (C, 4 * HD)) for g in range(4)], 0).astype(np.float32)
    bdmask = np.kron(np.eye(4, dtype=np.float32), np.ones((HD, HD), np.float32))
    return tri, hmask, rmask, ones_bd, gmask, bdmask


N_LEVELS = 4


def _dense_in_kernel(x_ref, g_ref, w_ref, z_ref):
    h = _rmsnorm(x_ref[...], g_ref[...])
    z_ref[...] = jnp.dot(h.astype(BF16), w_ref[...], preferred_element_type=F32)


def _layer(a, l):
    return pl.BlockSpec((None,) + a.shape[1:], lambda i, _n=a.ndim - 1: (l,) + (0,) * _n)


def _whole(a):
    return pl.BlockSpec(a.shape, lambda i, _n=a.ndim: (0,) * _n)


def _dense_in(x, g, w, l):
    rows = x.shape[0]
    tile = min(ROW_TILE, rows)
    return pl.pallas_call(
        _dense_in_kernel,
        out_shape=jax.ShapeDtypeStruct((rows, D_IN), F32),
        grid=(rows // tile,),
        in_specs=[pl.BlockSpec((tile, D_MODEL), lambda i: (i, 0)), _layer(g, l), _layer(w, l)],
        out_specs=pl.BlockSpec((tile, D_IN), lambda i: (i, 0)),
        compiler_params=pltpu.CompilerParams(
            dimension_semantics=("arbitrary",), vmem_limit_bytes=TPU_V7X_VMEM_LIMIT),
        name="dense_in",
    )(x, g, w)


def _dense_out_kernel(x_ref, o_ref, p_ref, wo_ref, gg_ref, wg_ref, wp_ref, gf_ref, y_ref, *, final):
    x1 = x_ref[...] + jnp.dot(o_ref[...], wo_ref[...], preferred_element_type=F32)
    gate = _sigmoid(jnp.dot(_rmsnorm(x1, gg_ref[...]).astype(BF16), wg_ref[...], preferred_element_type=F32))
    x2 = x1 + jnp.dot(p_ref[...].astype(BF16), wp_ref[...], preferred_element_type=F32) * gate
    if final:
        x2 = _rmsnorm(x2, gf_ref[...])
    y_ref[...] = x2


def _dense_out(x, o, p, wo, gg, wg, wp, gf, l, final):
    rows = x.shape[0]
    tile = min(ROW_TILE, rows)
    row_spec = lambda n: pl.BlockSpec((tile, n), lambda i: (i, 0))
    p_spec = pl.BlockSpec((None, tile, D_PLE), lambda i: (l, i, 0))
    return pl.pallas_call(
        functools.partial(_dense_out_kernel, final=final),
        out_shape=jax.ShapeDtypeStruct((rows, D_MODEL), F32),
        grid=(rows // tile,),
        in_specs=[row_spec(D_MODEL), row_spec(D_MIX), p_spec, _layer(wo, l), _layer(gg, l), _layer(wg, l),
                  _layer(wp, l), _whole(gf)],
        out_specs=row_spec(D_MODEL),
        compiler_params=pltpu.CompilerParams(
            dimension_semantics=("arbitrary",), vmem_limit_bytes=TPU_V7X_VMEM_LIMIT),
        name="dense_out",
    )(x, o, p, wo, gg, wg, wp, gf)


def _s5_scan(bu_ref, sall_ref, lre, lim, s_re, s_im, nt, nb):
    lr = jnp.broadcast_to(lre, (nb, D_S5_STATE))
    li = jnp.broadcast_to(lim, (nb, D_S5_STATE))

    def step(t, carry):
        sr, si = carry
        r0 = pl.multiple_of(t * nb, nb)
        br = bu_ref[pl.ds(r0, nb), 0:D_S5_STATE]
        bi = bu_ref[pl.ds(r0, nb), D_S5_STATE:2 * D_S5_STATE]
        nr = lr * sr - li * si + br
        ni = lr * si + li * sr + bi
        sall_ref[pl.ds(r0, nb), 0:D_S5_STATE] = nr
        sall_ref[pl.ds(r0, nb), D_S5_STATE:2 * D_S5_STATE] = ni
        return nr, ni

    return lax.fori_loop(0, nt, step, (s_re, s_im))


def _s5_head(u, sall_ref, cblk_ref, d_ref, g1_ref, g2_ref):
    y = jnp.dot(sall_ref[...].astype(BF16), cblk_ref[...], preferred_element_type=F32) + d_ref[...] * u
    y = jax.nn.gelu(y)
    yb = y.astype(BF16)
    return jnp.dot(yb, g1_ref[...], preferred_element_type=F32) * _sigmoid(
        jnp.dot(yb, g2_ref[...], preferred_element_type=F32))


PACK = 4
PW = PACK * HD
SEQ_PAIRS = 2
S5_STEPS = 16


def _interleave(required, background=()):
    live = list(required)
    extra = list(background)
    while live:
        for entry in list(live) + list(extra):
            gen, k = entry
            for _ in range(k):
                try:
                    next(gen)
                except StopIteration:
                    (live if entry in live else extra).remove(entry)
                    break


def _mix_prompt_packed_kernel(z_ref, lb_ref, hgn_ref, mu_ref, w0_ref, a0_ref, kk_ref, ka_ref, rk_ref, gng_ref,
                              gnb_ref, lre_ref, lim_ref, d_ref, wup_ref, aup_ref, bblk_ref, cblk_ref, g1_ref, g2_ref,
                              tri_ref, hmask_ref, rmask_ref, ones_ref, gmask_ref, bdmask_ref,
                              o_ref, hg_ref, rw_ref, sh_ref, sre_ref, sim_ref,
                              hg_bd0, hg_bd1, hg_bd2, rw_bd0, rw_bd1, rw_bd2, upad, up, bu, sall, ypad):
    nb = z_ref.shape[0]
    C = CHUNK
    n_pairs = D_HEADS // 128
    hg_bd = (hg_bd0, hg_bd1, hg_bd2)
    rw_bd = (rw_bd0, rw_bd1, rw_bd2)
    step = pl.program_id(0)

    @pl.when(step == 0)
    def _():
        for ref in hg_bd + rw_bd:
            ref[...] = jnp.zeros(ref.shape, F32)
        sh_ref[...] = jnp.zeros(sh_ref.shape, F32)
        sre_ref[...] = jnp.zeros(sre_ref.shape, F32)
        sim_ref[...] = jnp.zeros(sim_ref.shape, F32)

    ones_bd = ones_ref[...]
    bdmask = bdmask_ref[...]
    row_is0 = lax.broadcasted_iota(jnp.int32, (C, D_SHIFT), 0) == 0
    low_half = lax.broadcasted_iota(jnp.int32, (C, D_HEADS), 0) % 8 < HG_BASE

    def bd(x):
        m = x.shape[1] // PW
        blocks = []
        for g in range(PACK):
            mask = gmask_ref[g]
            if m > 1:
                mask = jnp.concatenate([mask] * m, axis=1)
            blocks.append(x * mask)
        return jnp.concatenate(blocks, axis=0)

    def prep_stages(b, p):
        q = z_ref[b, :, C_Q:C_Q + D_HEADS]
        f = z_ref[b, :, C_F:C_F + D_HEADS]
        lbs = jnp.maximum(lb_ref[...], LB_FLOOR)
        logf = jnp.log(lbs + (1.0 - lbs) * _sigmoid(f))
        kf = (1.0 - lbs) * _sigmoid(-f)
        qf = _silu(q)
        yield
        cum = _sel_left(tri_ref[...], logf)
        p["qin"] = (qf * jnp.exp(cum)).astype(BF16)
        p["kend"] = (kf * jnp.exp(cum[C - 1:C, :] - cum)).astype(BF16)
        yield
        for l in range(N_LEVELS):
            m = C >> (l + 1)
            ref = jnp.concatenate([jnp.broadcast_to(cum[q0 + m - 1:q0 + m, :], (2 * m, D_HEADS))
                                   for q0 in range(0, C, 2 * m)], axis=0)
            e = jnp.exp(-jnp.abs(cum - ref))
            p["qe%d" % l] = (qf * e).astype(BF16)
            p["ke%d" % l] = (kf * e).astype(BF16)
            yield
        first = jnp.concatenate([jnp.broadcast_to(cum[q0:q0 + 1, :], (8, D_HEADS)) for q0 in range(0, C, 8)], axis=0)
        second = jnp.concatenate([jnp.broadcast_to(cum[q0 + HG_BASE:q0 + HG_BASE + 1, :], (8, D_HEADS))
                                  for q0 in range(0, C, 8)], axis=0)
        dq = cum - jnp.where(low_half, first, second)
        p["qe%d" % N_LEVELS] = (qf * jnp.exp(dq)).astype(BF16)
        p["ke%d" % N_LEVELS] = (kf * jnp.exp(-dq)).astype(BF16)
        p["gcol"] = jnp.exp(cum[C - 8:C, :].T)[:, 7:8]
        p["ib"] = z_ref[b, :, C_I:C_I + D_HEADS].astype(BF16)
        yield

        c = z_ref[b, :, C_RW:C_RW + D_SHIFT]
        prev = jnp.where(row_is0, sh_ref[pl.ds(b, 1), :], pltpu.roll(c, 1, axis=0))
        p["last"] = c[C - 1:C, :]
        cs = c + mu_ref[...] * (prev - c)
        r = cs[:, 0:D_HEADS]
        k = cs[:, D_HEADS:2 * D_HEADS]
        v = cs[:, 2 * D_HEADS:3 * D_HEADS]
        wd = cs[:, 3 * D_HEADS:3 * D_HEADS + LORA]
        ad = cs[:, 3 * D_HEADS + LORA:D_SHIFT]
        logw = -RW_DECAY_SCALE * _sigmoid(w0_ref[...] + _dot(jnp.tanh(wd), wup_ref[...]))
        a = _sigmoid(a0_ref[...] + _dot(ad, aup_ref[...]))
        yield
        kk = k * kk_ref[...]
        kk = kk * lax.rsqrt(jnp.maximum(_head_sum(kk * kk, ones_bd), 1e-24))
        kt = k * (1.0 + (a - 1.0) * ka_ref[...])
        beta = kk * a
        p["bonus"] = _head_sum(r * kt * rk_ref[...], ones_bd)
        cl = _sel_left(tri_ref[...], logw)
        yield
        p["aq"] = kk * jnp.exp(cl - logw)
        p["rq"] = r * jnp.exp(cl)
        ecl = jnp.exp(-cl)
        p["kd"] = (kt * ecl).astype(BF16)
        p["bd"] = (beta * ecl).astype(BF16)
        eend = jnp.exp(cl[C - 1:C, :] - cl)
        p["ktl"] = kt * eend
        p["btl"] = beta * eend
        p["gam"] = jnp.exp(cl[C - 1:C, :])
        p["v"] = v

    def chain_stages(i, ps):
        b0 = 2 * SEQ_PAIRS * i
        pairs = range(SEQ_PAIRS * n_pairs)
        pj = [u % n_pairs for u in pairs]
        pq = [2 * (u // n_pairs) for u in pairs]
        tiles = [slice(pj[u] * 128, (pj[u] + 1) * 128) for u in pairs]
        slot = [SEQ_PAIRS * i + u // n_pairs for u in pairs]

        def x4(name, rows=False):
            if rows:
                return [jnp.concatenate([ps[pq[u]][name][tiles[u], :], ps[pq[u] + 1][name][tiles[u], :]], axis=0)
                        for u in pairs]
            return [jnp.concatenate([ps[pq[u]][name][:, tiles[u]], ps[pq[u] + 1][name][:, tiles[u]]], axis=1)
                    for u in pairs]

        s_hg = [hg_bd[pj[u]][slot[u]] for u in pairs]
        s_rw = [rw_bd[pj[u]][slot[u]] for u in pairs]
        aq4, rq4, v4, btl4, ktl4 = x4("aq"), x4("rq"), x4("v"), x4("btl"), x4("ktl")
        bd4, kd4 = x4("bd"), x4("kd")
        ar = [jnp.concatenate([aq4[j], rq4[j]], axis=0).astype(BF16) for j in pairs]
        zb = [_dot_nt(ar[j], bd(bd4[j])) for j in pairs]
        zk = [_dot_nt(ar[j], bd(kd4[j])) for j in pairs]
        yield
        mb = [zb[j][0:C] * rmask_ref[0] for j in pairs]
        nbm = [zb[j][C:2 * C] * rmask_ref[1] for j in pairs]
        mk = [zk[j][0:C] * rmask_ref[0] for j in pairs]
        nk = [zk[j][C:2 * C] * rmask_ref[1] for j in pairs]
        bdv = [bd(v4[j].astype(BF16)) for j in pairs]
        x0 = [jnp.concatenate([aq4[j], _dot(mk[j], bdv[j])], axis=1) for j in pairs]
        yield
        bm = [-m for m in mb]
        pw = [_dot(mb[j], bd(mb[j].astype(BF16))) for j in pairs]
        att = [None for _ in pairs]
        level = 0
        n = 2
        while 2 * n < C:
            yield
            both = [_dot(jnp.concatenate([bm[j], pw[j]], axis=0), bd(pw[j].astype(BF16))) for j in pairs]
            bm = [bm[j] + pw[j] + both[j][0:C] for j in pairs]
            pw = [both[j][C:2 * C] for j in pairs]
            n *= 2
            qe4, ke4 = x4("qe%d" % level), x4("ke%d" % level)
            for j in pairs:
                term = hmask_ref[level] * _dot_nt(qe4[j], bd(ke4[j]))
                att[j] = term if att[j] is None else att[j] + term
            level += 1
        yield
        bm = [bm[j] + pw[j] + _dot(bm[j], bd(pw[j].astype(BF16))) for j in pairs]
        while level <= N_LEVELS:
            qe4, ke4 = x4("qe%d" % level), x4("ke%d" % level)
            for j in pairs:
                term = hmask_ref[level] * _dot_nt(qe4[j], bd(ke4[j]))
                att[j] = term if att[j] is None else att[j] + term
            level += 1
        yield
        ib4, qin4, kend4 = x4("ib"), x4("qin"), x4("kend")
        o4 = [_dot(att[j], bd(ib4[j])) + _dot(qin4[j], s_hg[j]) for j in pairs]
        gcol4 = x4("gcol", rows=True)
        new_hg = [gcol4[j] * s_hg[j] + bdmask * _dot_tn(kend4[j], ib4[j]) for j in pairs]
        x = [x0[j] + _dot(bm[j], bd(x0[j].astype(BF16))) for j in pairs]
        yield
        bdx = [bd(x[j].astype(BF16)) for j in pairs]
        w = [bdmask * _dot_tn(btl4[j], x[j][:, 0:PW]) for j in pairs]
        gt = [bdmask * _dot_tn(jnp.concatenate([v4[j], x[j][:, PW:2 * PW]], axis=0),
                               jnp.concatenate([ktl4[j], -btl4[j]], axis=0)) for j in pairs]
        nx = [_dot(nbm[j], bdx[j]) for j in pairs]
        yield
        y4 = [_dot_nt(rq4[j] - nx[j][:, 0:PW], s_rw[j]) + _dot(nk[j], bdv[j]) - nx[j][:, PW:2 * PW] for j in pairs]
        gam4 = x4("gam")
        new_rw = [s_rw[j] * gam4[j] - _dot_nt(s_rw[j], w[j]) + gt[j] for j in pairs]
        for j in pairs:
            hg_bd[pj[j]][slot[j]] = new_hg[j]
            rw_bd[pj[j]][slot[j]] = new_rw[j]
        yield
        o_tiles = [[o4[(u // 2) * n_pairs + j][:, (u % 2) * 128:(u % 2 + 1) * 128] for j in range(n_pairs)]
                   for u in range(2 * SEQ_PAIRS)]
        y_tiles = [[y4[(u // 2) * n_pairs + j][:, (u % 2) * 128:(u % 2 + 1) * 128] for j in range(n_pairs)]
                   for u in range(2 * SEQ_PAIRS)]
        for u in range(2 * SEQ_PAIRS):
            b = b0 + u
            o = jnp.concatenate(o_tiles[u], axis=1)
            ssq = _head_sum(o * o, ones_bd)
            o = o * lax.rsqrt(ssq * (1.0 / HD) + NORM_EPS) * hgn_ref[...]
            o_ref[b, :, 0:D_HEADS] = (o * _silu(z_ref[b, :, C_GA:C_GA + D_HEADS])).astype(BF16)
            y = jnp.concatenate(y_tiles[u], axis=1)
            yc = y - _head_sum(y, ones_bd) * (1.0 / HD)
            var = _head_sum(yc * yc, ones_bd) * (1.0 / HD)
            yn = yc * lax.rsqrt(var + RW_GN_EPS) * gng_ref[...] + gnb_ref[...] + ps[u]["bonus"] * ps[u]["v"]
            o_ref[b, :, D_HEADS:2 * D_HEADS] = (yn * _silu(z_ref[b, :, C_GB:C_GB + D_HEADS])).astype(BF16)
            sh_ref[pl.ds(b, 1), :] = ps[u]["last"]
            yield

    n_lane = D_S5 // 128
    n_groups = nb // (2 * SEQ_PAIRS)
    s5_blocks = C // S5_STEPS // n_groups
    blk_rows = S5_STEPS * nb

    def s5_stages(i, state):
        lr = jnp.broadcast_to(lre_ref[...], (nb, D_S5_STATE))
        li = jnp.broadcast_to(lim_ref[...], (nb, D_S5_STATE))
        sr, si = state
        for k in range(s5_blocks):
            t0 = (i * s5_blocks + k) * S5_STEPS
            row0 = pl.multiple_of(t0 * nb, blk_rows)
            for tt in range(S5_STEPS):
                for j in range(n_lane):
                    up[pl.ds(pl.multiple_of(row0 + tt * nb, nb), nb), j * 128:(j + 1) * 128] = (
                        upad[j, pl.ds(t0 + tt, nb, stride=PITCH), :])
            u = up[pl.ds(row0, blk_rows), :]
            bu[pl.ds(row0, blk_rows), :] = jnp.dot(u.astype(BF16), bblk_ref[...], preferred_element_type=F32)
            yield
            for tt in range(S5_STEPS):
                r0 = pl.multiple_of(row0 + tt * nb, nb)
                br = bu[pl.ds(r0, nb), 0:D_S5_STATE]
                bi = bu[pl.ds(r0, nb), D_S5_STATE:2 * D_S5_STATE]
                sr, si = lr * sr - li * si + br, lr * si + li * sr + bi
                sall[pl.ds(r0, nb), 0:D_S5_STATE] = sr
                sall[pl.ds(r0, nb), D_S5_STATE:2 * D_S5_STATE] = si
                if tt % 4 == 3:
                    yield
            y = jnp.dot(sall[pl.ds(row0, blk_rows), :].astype(BF16), cblk_ref[...],
                        preferred_element_type=F32) + d_ref[...] * u
            yb = jax.nn.gelu(y).astype(BF16)
            yg = jnp.dot(yb, g1_ref[...], preferred_element_type=F32) * _sigmoid(
                jnp.dot(yb, g2_ref[...], preferred_element_type=F32))
            for tt in range(S5_STEPS):
                for j in range(n_lane):
                    ypad[j, pl.ds(t0 + tt, nb, stride=PITCH), :] = yg[tt * nb:(tt + 1) * nb, j * 128:(j + 1) * 128]
            yield
        state[0], state[1] = sr, si

    for b in range(nb):
        for j in range(n_lane):
            upad[j, b * PITCH:b * PITCH + C, :] = z_ref[b, :, C_U + j * 128:C_U + (j + 1) * 128]

    def group_body(i, carry):
        ps = [{} for _ in range(2 * SEQ_PAIRS)]
        state = list(carry)
        s5 = (s5_stages(i, state), 1)
        _interleave([(prep_stages(2 * SEQ_PAIRS * i + q, ps[q]), 1) for q in range(2 * SEQ_PAIRS)])
        _interleave([(chain_stages(i, ps), 1)], background=[s5])
        _interleave([s5])
        return tuple(state)

    s_re, s_im = lax.fori_loop(0, n_groups, group_body, (sre_ref[...], sim_ref[...]))
    sre_ref[...] = s_re
    sim_ref[...] = s_im
    for b in range(nb):
        for j in range(n_lane):
            gate = _silu(z_ref[b, :, C_GC + j * 128:C_GC + (j + 1) * 128])
            o_ref[b, :, 2 * D_HEADS + j * 128:2 * D_HEADS + (j + 1) * 128] = (
                ypad[j, b * PITCH:b * PITCH + C, :] * gate).astype(BF16)

    @pl.when(step == pl.num_programs(0) - 1)
    def _():
        for i in range(nb // 2):
            for j in range(n_pairs):
                for g in range(PACK):
                    b = 2 * i + g // 2
                    h = 2 * j + g % 2
                    blk = slice(g * HD, (g + 1) * HD)
                    hg_ref[b, h] = hg_bd[j][i, blk, :][:, blk]
                    rw_ref[b, h] = rw_bd[j][i, blk, :][:, blk]


def _mix_prompt(z, vecs, mats, consts, l):
    B, L, _ = z.shape
    nt = L // CHUNK
    out_shape = (
        jax.ShapeDtypeStruct((B, L, D_MIX), BF16),
        jax.ShapeDtypeStruct((B, HEADS, HD, HD), F32),
        jax.ShapeDtypeStruct((B, HEADS, HD, HD), F32),
        jax.ShapeDtypeStruct((B, D_SHIFT), F32),
        jax.ShapeDtypeStruct((B, D_S5_STATE), F32),
        jax.ShapeDtypeStruct((B, D_S5_STATE), F32),
    )
    out_specs = (
        pl.BlockSpec((B, CHUNK, D_MIX), lambda i: (0, i, 0)),
        pl.BlockSpec((B, HEADS, HD, HD), lambda i: (0, 0, 0, 0)),
        pl.BlockSpec((B, HEADS, HD, HD), lambda i: (0, 0, 0, 0)),
        pl.BlockSpec((B, D_SHIFT), lambda i: (0, 0)),
        pl.BlockSpec((B, D_S5_STATE), lambda i: (0, 0)),
        pl.BlockSpec((B, D_S5_STATE), lambda i: (0, 0)),
    )
    args = (z,) + tuple(vecs) + tuple(mats) + tuple(consts)
    in_specs = ([pl.BlockSpec((B, CHUNK, D_IN), lambda i: (0, i, 0))]
                + [_layer(a, l) for a in tuple(vecs) + tuple(mats)] + [_whole(a) for a in consts])
    rows = B * CHUNK
    assert B % (2 * SEQ_PAIRS) == 0 and D_HEADS // 128 == 3
    scratch = [pltpu.VMEM((B // 2, PW, PW), F32) for _ in range(6)] + [
        pltpu.VMEM((D_S5 // 128, B * PITCH, 128), F32),
        pltpu.VMEM((rows, D_S5), F32),
        pltpu.VMEM((rows, 2 * D_S5_STATE), F32),
        pltpu.VMEM((rows, 2 * D_S5_STATE), F32),
        pltpu.VMEM((D_S5 // 128, B * PITCH, 128), F32),
    ]
    return pl.pallas_call(
        _mix_prompt_packed_kernel,
        out_shape=out_shape,
        grid=(nt,),
        in_specs=in_specs,
        out_specs=out_specs,
        scratch_shapes=scratch,
        compiler_params=pltpu.CompilerParams(
            dimension_semantics=("arbitrary",), vmem_limit_bytes=TPU_V7X_VMEM_LIMIT),
        name="mix_prompt",
    )(*args)


def _mix_sample_lanes_kernel(zq, zf, zi, zga, zr, zk, zv, zwd, zad, zgb, pr, pk, pv, pwd, pad_,
                             mur, muk, muv, muwd, muad, lb_ref, hgn_ref, w0_ref, a0_ref, kk_ref, ka_ref, rk_ref,
                             gng_ref, gnb_ref, wupT_ref, aupT_ref, hg_in, rw_in, hg_prev, rw_prev,
                             oa_ref, ob_ref, hg_ref, rw_ref,
                             fg_s, kf_s, qf_s, iv_s, kk_s, dec_s, beta_s, kt_s, r_s, v_s, y_s):
    del hg_prev, rw_prev
    nb = pr.shape[1]
    T = zq.shape[1] // nb
    lbs = jnp.maximum(lb_ref[...], LB_FLOOR)
    col = lambda ref, t: ref[:, t * nb:(t + 1) * nb]

    def shifted(z, p, mu, t):
        c = col(z, t)
        prev = p[...] if t == 0 else col(z, t - 1)
        return c + mu[...] * (prev - c)

    for t in range(T):
        f = col(zf, t)
        fg_s[t] = lbs + (1.0 - lbs) * _sigmoid(f)
        kf_s[t] = (1.0 - lbs) * _sigmoid(-f)
        qf_s[t] = _silu(col(zq, t))
        iv_s[t] = col(zi, t)
        r = shifted(zr, pr, mur, t)
        k = shifted(zk, pk, muk, t)
        v = shifted(zv, pv, muv, t)
        wd = shifted(zwd, pwd, muwd, t)
        ad = shifted(zad, pad_, muad, t)
        logw = -RW_DECAY_SCALE * _sigmoid(w0_ref[...] + _dot(wupT_ref[...], jnp.tanh(wd)))
        a = _sigmoid(a0_ref[...] + _dot(aupT_ref[...], ad))
        kk = k * kk_ref[...]
        kk = kk * lax.rsqrt(jnp.maximum(jnp.sum(kk * kk, axis=0, keepdims=True), 1e-24))
        kk_s[t] = kk
        dec_s[t] = jnp.exp(logw)
        beta_s[t] = kk * a
        kt_s[t] = k * (1.0 + (a - 1.0) * ka_ref[...])
        r_s[t] = r
        v_s[t] = v

    def hg_row(k, acc):
        s = hg_in[k]
        acc = list(acc)
        for t in range(T):
            row = lambda ref: ref[t, pl.ds(k, 1), :]
            s = row(fg_s) * s + row(kf_s) * iv_s[t]
            acc[t] = acc[t] + row(qf_s) * s
        hg_ref[k] = s
        return tuple(acc)

    zero = jnp.zeros((HD, nb), F32)
    o = lax.fori_loop(0, HD, hg_row, (zero,) * T)
    for t in range(T):
        ssq = jnp.sum(o[t] * o[t], axis=0, keepdims=True)
        on = o[t] * lax.rsqrt(ssq * (1.0 / HD) + NORM_EPS) * hgn_ref[...]
        oa_ref[:, t * nb:(t + 1) * nb] = (on * _silu(col(zga, t))).astype(BF16)

    def rw_row(vi, carry):
        s = rw_in[vi]
        for t in range(T):
            sa = jnp.sum(s * kk_s[t], axis=0, keepdims=True)
            s = s * dec_s[t] - sa * beta_s[t] + v_s[t, pl.ds(vi, 1), :] * kt_s[t]
            y_s[t, pl.ds(vi, 1), :] = jnp.sum(s * r_s[t], axis=0, keepdims=True)
        rw_ref[vi] = s
        return carry

    lax.fori_loop(0, HD, rw_row, 0, unroll=4)
    for t in range(T):
        y = y_s[t]
        yc = y - jnp.mean(y, axis=0, keepdims=True)
        var = jnp.mean(yc * yc, axis=0, keepdims=True)
        bonus = jnp.sum(r_s[t] * kt_s[t] * rk_ref[...], axis=0, keepdims=True)
        yn = yc * lax.rsqrt(var + RW_GN_EPS) * gng_ref[...] + gnb_ref[...] + bonus * v_s[t]
        ob_ref[:, t * nb:(t + 1) * nb] = (yn * _silu(col(zgb, t))).astype(BF16)


def _mix_sample_lanes(zT, shT, hgT, rwT, carried, cols, mu_cols, wupT, aupT, l):
    B = shT.shape[2]
    TB = zT.shape[1]
    zblk = lambda off: pl.BlockSpec((HD, TB), lambda h: (off // HD + h, 0))
    zone = lambda off: pl.BlockSpec((HD, TB), lambda h: (off // HD, 0))
    sblk = lambda off: pl.BlockSpec((None, HD, B), lambda h: (l, off // HD + h, 0))
    sone = lambda off: pl.BlockSpec((None, HD, B), lambda h: (l, off // HD, 0))
    r0, k0, v0 = 0, D_HEADS, 2 * D_HEADS
    wd0, ad0 = 3 * D_HEADS, 3 * D_HEADS + LORA
    state = pl.BlockSpec((None, None, HD, HD, B), lambda h: (l, h, 0, 0, 0))
    head_cols = pl.BlockSpec((None, HD, B), lambda h: (l, h, 0))
    lora = pl.BlockSpec((None, HD, LORA), lambda h: (l, h, 0))
    in_specs = ([zblk(C_Q), zblk(C_F), zblk(C_I), zblk(C_GA), zblk(C_RW + r0), zblk(C_RW + k0), zblk(C_RW + v0),
                 zone(C_RW + wd0), zone(C_RW + ad0), zblk(C_GB)]
                + [sblk(r0), sblk(k0), sblk(v0), sone(wd0), sone(ad0)]
                + [sblk(r0), sblk(k0), sblk(v0), sone(wd0), sone(ad0)]
                + [head_cols] * len(cols) + [lora, lora, state, state]
                + [pl.BlockSpec(memory_space=pl.ANY)] * 2)
    out_row = pl.BlockSpec((HD, TB), lambda h: (h, 0))
    slab = pltpu.VMEM((TB // B, HD, B), F32)
    n_in = len(in_specs)
    return pl.pallas_call(
        _mix_sample_lanes_kernel,
        out_shape=(
            jax.ShapeDtypeStruct((D_HEADS, TB), BF16),
            jax.ShapeDtypeStruct((D_HEADS, TB), BF16),
            jax.ShapeDtypeStruct(hgT.shape, F32),
            jax.ShapeDtypeStruct(rwT.shape, F32),
        ),
        grid=(HEADS,),
        in_specs=in_specs,
        out_specs=(out_row, out_row, state, state),
        input_output_aliases={n_in - 2: 2, n_in - 1: 3},
        scratch_shapes=[slab] * 11,
        compiler_params=pltpu.CompilerParams(
            dimension_semantics=("arbitrary",), vmem_limit_bytes=TPU_V7X_VMEM_LIMIT),
        name="mix_sample",
    )(*([zT] * 10 + [shT] * 5 + [mu_cols] * 5 + list(cols) + [wupT, aupT, hgT, rwT] + list(carried)))


def _s5_sample_kernel(z_ref, sre_in, sim_in, lre_ref, lim_ref, d_ref, bblk_ref, cblk_ref, g1_ref, g2_ref,
                      o_ref, sre_ref, sim_ref, bu, sall, *, nt, nb):
    u = z_ref[:, C_U:C_U + D_S5]
    bu[...] = jnp.dot(u.astype(BF16), bblk_ref[...], preferred_element_type=F32)
    s_re, s_im = _s5_scan(bu, sall, lre_ref[...], lim_ref[...], sre_in[...], sim_in[...], nt, nb)
    sre_ref[...] = s_re
    sim_ref[...] = s_im
    yg = _s5_head(u, sall, cblk_ref, d_ref, g1_ref, g2_ref)
    o_ref[...] = (yg * _silu(z_ref[:, C_GC:C_GC + D_S5])).astype(BF16)


def _s5_sample(z2d, sre, sim, stacked, nt, nb, l):
    rows = z2d.shape[0]
    return pl.pallas_call(
        functools.partial(_s5_sample_kernel, nt=nt, nb=nb),
        out_shape=(
            jax.ShapeDtypeStruct((rows, D_S5), BF16),
            jax.ShapeDtypeStruct(sre.shape, F32),
            jax.ShapeDtypeStruct(sim.shape, F32),
        ),
        grid=(1,),
        in_specs=[_whole(z2d), _layer(sre, l), _layer(sim, l)] + [_layer(a, l) for a in stacked],
        out_specs=(pl.BlockSpec((rows, D_S5), lambda i: (0, 0)), _layer(sre, l), _layer(sim, l)),
        input_output_aliases={1: 1, 2: 2},
        scratch_shapes=[pltpu.VMEM((rows, 2 * D_S5_STATE), F32), pltpu.VMEM((rows, 2 * D_S5_STATE), F32)],
        compiler_params=pltpu.CompilerParams(
            dimension_semantics=("arbitrary",), vmem_limit_bytes=TPU_V7X_VMEM_LIMIT),
        name="s5_sample",
    )(z2d, sre, sim, *stacked)


def _prep_kernel(lbraw_ref, are_ref, aim_ref, ldt_ref, bre_ref, bim_ref, lb_ref, lre_ref, lim_ref, bbre_ref, bbim_ref):
    raw = lbraw_ref[...]
    e = jnp.exp(raw - jnp.max(raw, axis=0, keepdims=True))
    sm = e / jnp.sum(e, axis=0, keepdims=True)
    depth = raw.shape[0]
    acc = jnp.zeros_like(sm[0:1])
    for l in range(depth):
        acc = acc + sm[l:l + 1]
        lb_ref[l:l + 1, :] = acc - sm[0:1]
    a_re = are_ref[...]
    a_im = aim_ref[...]
    dt = jnp.exp(ldt_ref[...])
    mag = jnp.exp(dt * a_re)
    lam_re = mag * jnp.cos(dt * a_im)
    lam_im = mag * jnp.sin(dt * a_im)
    den = a_re * a_re + a_im * a_im
    xr = lam_re - 1.0
    f_re = (xr * a_re + lam_im * a_im) / den
    f_im = (lam_im * a_re - xr * a_im) / den
    lre_ref[...] = lam_re
    lim_ref[...] = lam_im
    b_re = bre_ref[...]
    b_im = bim_ref[...]
    bbre_ref[...] = f_re * b_re - f_im * b_im
    bbim_ref[...] = f_re * b_im + f_im * b_re


def _prep(hg_lb_raw, s5_a_re, s5_a_im, s5_log_dt, s5_b_re, s5_b_im):
    depth = hg_lb_raw.shape[0]
    n = depth * S5_GROUPS * S5_P
    col = lambda a: a.reshape(n, 1)
    ldt = jnp.broadcast_to(s5_log_dt[:, :, None], (depth, S5_GROUPS, S5_P))
    lb, lre, lim, bbre, bbim = pl.pallas_call(
        _prep_kernel,
        out_shape=(
            jax.ShapeDtypeStruct((depth, D_HEADS), F32),
            jax.ShapeDtypeStruct((n, 1), F32),
            jax.ShapeDtypeStruct((n, 1), F32),
            jax.ShapeDtypeStruct((n, S5_CH), F32),
            jax.ShapeDtypeStruct((n, S5_CH), F32),
        ),
        name="param_prep",
    )(hg_lb_raw, col(s5_a_re), col(s5_a_im), col(ldt), s5_b_re.reshape(n, S5_CH), s5_b_im.reshape(n, S5_CH))
    shape3 = (depth, S5_GROUPS, S5_P)
    return (lb, lre.reshape(depth, 1, D_S5_STATE), lim.reshape(depth, 1, D_S5_STATE),
            bbre.reshape(shape3 + (S5_CH,)), bbim.reshape(shape3 + (S5_CH,)))


def kernel(x_prompt, x_sample, state_hgrn, state_rwkv, state_rwkv_shift, state_s5_re, state_s5_im, p_prompt, p_sample, g_in, w_in, hg_lb_raw, hg_norm_g, rw_mu, rw_w0, rw_w_up, rw_a0, rw_a_up, rw_k_k, rw_k_a, rw_r_k, rw_gn_g, rw_gn_b, s5_a_re, s5_a_im, s5_log_dt, s5_b_re, s5_b_im, s5_c_re, s5_c_im, s5_d, s5_glu_w1, s5_glu_w2, w_out, ple_w_proj, ple_gate_g, ple_w_gate, g_final):
    depth = w_in.shape[0]
    B, L, _ = x_prompt.shape
    BS, T, _ = x_sample.shape
    tri, hmask, rmask, ones_np, gmask, bdmask = _chunk_consts()
    ones_bd = jnp.asarray(ones_np, BF16)
    consts = (jnp.asarray(tri, BF16), jnp.asarray(hmask, F32), jnp.asarray(rmask, F32), ones_bd,
              jnp.asarray(gmask, BF16), jnp.asarray(bdmask, F32))

    lb_all, lam_re, lam_im, bb_re, bb_im = _prep(hg_lb_raw, s5_a_re, s5_a_im, s5_log_dt, s5_b_re, s5_b_im)
    g_final2 = g_final.reshape(1, -1)

    xp = x_prompt.reshape(B * L, D_MODEL)
    xs = jnp.transpose(x_sample, (1, 0, 2)).reshape(T * BS, D_MODEL)
    pp = p_prompt.reshape(depth, B * L, D_PLE)
    ps = jnp.transpose(p_sample, (0, 2, 1, 3)).reshape(depth, T * BS, D_PLE)

    rows3 = lambda a: a.reshape(depth, 1, -1)
    cast = lambda a: a.astype(BF16)
    g_in3, gate_g3 = rows3(g_in), rows3(ple_gate_g)
    w_in_b, wo, wg, wp = cast(w_in), cast(w_out), cast(ple_w_gate), cast(ple_w_proj)
    wup, aup, g1, g2 = cast(rw_w_up), cast(rw_a_up), cast(s5_glu_w1), cast(s5_glu_w2)
    eye = jnp.eye(S5_GROUPS, dtype=F32)
    bd_in = lambda bb: jnp.einsum('lgpc,gh->lgchp', bb, eye).reshape(depth, D_S5, D_S5_STATE)
    bd_out = lambda c: jnp.einsum('lgcp,gh->lgphc', c, eye).reshape(depth, D_S5_STATE, D_S5)
    bblk = cast(jnp.concatenate([bd_in(bb_re), bd_in(bb_im)], axis=2))
    cblk = cast(jnp.concatenate([bd_out(s5_c_re), -bd_out(s5_c_im)], axis=1))
    hr_vecs = tuple(rows3(a) for a in (lb_all, hg_norm_g, rw_mu, rw_w0, rw_a0, rw_k_k, rw_k_a, rw_r_k,
                                       rw_gn_g, rw_gn_b))
    s5_vecs = (lam_re, lam_im, rows3(s5_d))

    hg_t = jnp.transpose(state_hgrn, (0, 2, 3, 4, 1))
    rw_t = jnp.transpose(state_rwkv, (0, 2, 3, 4, 1))
    sh_t = jnp.transpose(state_rwkv_shift, (0, 2, 1))
    lanes = lambda a: jnp.broadcast_to(a[:, :, None], a.shape + (BS,))
    s_cols = tuple(lanes(a) for a in (lb_all, hg_norm_g, rw_w0, rw_a0, rw_k_k, rw_k_a, rw_r_k, rw_gn_g, rw_gn_b))
    mu_cols = lanes(rw_mu)
    wup_t = cast(jnp.transpose(rw_w_up, (0, 2, 1)))
    aup_t = cast(jnp.transpose(rw_a_up, (0, 2, 1)))
    carried = (jnp.zeros(hg_t.shape, F32), jnp.zeros(rw_t.shape, F32))
    sh_out = []
    sre_s = state_s5_re.reshape(depth, BS, D_S5_STATE)
    sim_s = state_s5_im.reshape(depth, BS, D_S5_STATE)
    outs_p = [[] for _ in range(5)]
    for l in range(depth):
        final = l == depth - 1

        z = _dense_in(xp, g_in3, w_in_b, l)
        o, hg, rw, sh, sre, sim = _mix_prompt(
            z.reshape(B, L, D_IN), hr_vecs + s5_vecs, (wup, aup, bblk, cblk, g1, g2), consts, l)
        xp = _dense_out(xp, o.reshape(B * L, D_MIX), pp, wo, gate_g3, wg, wp, g_final2, l, final)
        for acc, val in zip(outs_p, (hg, rw, sh, sre.reshape(B, S5_GROUPS, S5_P), sim.reshape(B, S5_GROUPS, S5_P))):
            acc.append(val)

        zs = _dense_in(xs, g_in3, w_in_b, l)
        oa_t, ob_t, *carried = _mix_sample_lanes(zs.T, sh_t, hg_t, rw_t, tuple(carried), s_cols, mu_cols,
                                                 wup_t, aup_t, l)
        sh_out.append(zs[(T - 1) * BS:, C_RW:C_RW + D_SHIFT])
        o_c, sre_s, sim_s = _s5_sample(zs, sre_s, sim_s, s5_vecs + (bblk, cblk, g1, g2), T, BS, l)
        os_ = jnp.concatenate([oa_t.T, ob_t.T, o_c], axis=1)
        xs = _dense_out(xs, os_, ps, wo, gate_g3, wg, wp, g_final2, l, final)

    y_prompt = xp.reshape(B, L, D_MODEL)
    y_sample = jnp.transpose(xs.reshape(T, BS, D_MODEL), (1, 0, 2))
    s5_shape = (depth, BS, S5_GROUPS, S5_P)
    hg_s, rw_s = (jnp.transpose(a, (0, 4, 1, 2, 3)) for a in carried)
    return ((y_prompt, y_sample) + tuple(jnp.stack(a) for a in outs_p)
            + (hg_s, rw_s, jnp.stack(sh_out), sre_s.reshape(s5_shape), sim_s.reshape(s5_shape)))
```
